```python
import functools
import jax, jax.numpy as jnp
from jax import lax
import numpy as np

D_MODEL = 1024
BATCH = 16
SEQ = 4096
DEPTH = 1

GRID_W = 64
CTX_LEN = 256
POOL_WIDTH = 512
POOL_WINDOWS = (2, 4, 8, 16)
POOL_GROUPS = len(POOL_WINDOWS)
POOL_GC = POOL_WIDTH // POOL_GROUPS
N_HEADS = 16
N_KV_HEADS = 4
GROUP = N_HEADS // N_KV_HEADS
HEAD_DIM = 64
ATTN_WIDTH = N_HEADS * HEAD_DIM
KV_WIDTH = N_KV_HEADS * HEAD_DIM
WINDOW = 128
Q_BLOCK = 128
ROPE_BASE = 10000.0
ROPE_AXIS_DIM = HEAD_DIM // 2
EPS = 1e-6
NEG_INF = -1e30
SPLITS = (POOL_WIDTH, POOL_WIDTH, ATTN_WIDTH, KV_WIDTH, KV_WIDTH, ATTN_WIDTH, 2 * D_MODEL)
IN_WIDTH = POOL_WIDTH * 2 + ATTN_WIDTH * 2 + KV_WIDTH * 2 + 2 * D_MODEL
K_OFF = 2 * POOL_WIDTH + ATTN_WIDTH
V_OFF = K_OFF + KV_WIDTH

kernel_name = "hybrid_pool_window_gqa_dit_block"


def rms_norm(x, g):
    xf = x.astype(jnp.float32)
    y = xf * lax.rsqrt(jnp.mean(xf * xf, axis=-1, keepdims=True) + EPS)
    return (y * g.astype(jnp.float32)).astype(x.dtype)


def adaln(cond, w_mod, b_mod):
    m = jax.nn.silu(cond) @ w_mod + b_mod
    shift, scale, gate = jnp.split(m, 3, axis=-1)
    return shift, scale, gate


def modulate(xn, shift, scale):
    return xn * (1.0 + scale) + shift


def split_columns(p):
    outs, off = [], 0
    for w in SPLITS:
        outs.append(p[..., off:off + w])
        off += w
    return outs


def rope_tables(pos):
    n_freq = ROPE_AXIS_DIM // 2
    freqs = ROPE_BASE ** (-jnp.arange(n_freq, dtype=jnp.float32) / n_freq)
    ang = pos.astype(jnp.float32)[:, None] * freqs[None, :]
    return jnp.cos(ang), jnp.sin(ang)


def _rotate(xh, cos, sin):
    half = xh.shape[-1] // 2
    x1, x2 = xh[..., :half], xh[..., half:]
    return jnp.concatenate([x1 * cos - x2 * sin, x2 * cos + x1 * sin], axis=-1)


def apply_axial_rope(x, cos_r, sin_r, cos_c, sin_c):
    shp = (1, x.shape[1]) + (1,) * (x.ndim - 3) + (ROPE_AXIS_DIM // 2,)
    r = lambda t: t.reshape(shp).astype(x.dtype)
    return jnp.concatenate([_rotate(x[..., :ROPE_AXIS_DIM], r(cos_r), r(sin_r)),
                            _rotate(x[..., ROPE_AXIS_DIM:], r(cos_c), r(sin_c))], axis=-1)


def pool_mixer(xa, pool_w, pool_scale):
    B, L, _ = xa.shape
    xg = xa.reshape(B, L, POOL_GROUPS, POOL_GC)
    csum = jnp.cumsum(xg.astype(jnp.float32), axis=1)
    P = jnp.concatenate([jnp.zeros_like(csum[:, :1]), csum], axis=1)
    t = jnp.arange(L)
    outs = []
    for g, w in enumerate(POOL_WINDOWS):
        lo = jnp.clip(t - w // 2, 0, L)
        hi = jnp.clip(t + w // 2, 0, L)
        cnt = (hi - lo).astype(jnp.float32)[None, :, None]
        mean = (P[:, hi, g] - P[:, lo, g]) / cnt
        outs.append(mean - xg[:, :, g].astype(jnp.float32))
    pooled = jnp.stack(outs, axis=2).astype(xa.dtype)
    y = jnp.einsum('bsgc,gcd->bsgd', pooled, pool_w).reshape(B, L, POOL_WIDTH)
    return y * pool_scale


def windowed_attention(q, k, v, k_ctx, v_ctx, sink):
    B, L = q.shape[:2]
    C = k_ctx.shape[1]
    nblk = L // Q_BLOCK
    scale = HEAD_DIM ** -0.5
    pad = ((0, 0), (Q_BLOCK, Q_BLOCK), (0, 0), (0, 0))
    kp = jnp.pad(k, pad)
    vp = jnp.pad(v, pad)
    sink_col = jnp.broadcast_to(sink.astype(jnp.float32).reshape(1, N_KV_HEADS, GROUP, 1, 1),
                                (B, N_KV_HEADS, GROUP, Q_BLOCK, 1))

    def block(i):
        start = i * Q_BLOCK
        qb = lax.dynamic_slice_in_dim(q, start, Q_BLOCK, axis=1)
        kb = lax.dynamic_slice_in_dim(kp, start, 3 * Q_BLOCK, axis=1)
        vb = lax.dynamic_slice_in_dim(vp, start, 3 * Q_BLOCK, axis=1)
        qpos = start + jnp.arange(Q_BLOCK)
        kpos = start - Q_BLOCK + jnp.arange(3 * Q_BLOCK)
        valid = ((jnp.abs(qpos[:, None] - kpos[None, :]) <= WINDOW)
                 & (kpos >= 0)[None, :] & (kpos < L)[None, :])
        s_lat = jnp.einsum('bqhgd,bkhd->bhgqk', qb, kb).astype(jnp.float32) * scale
        s_lat = jnp.where(valid, s_lat, NEG_INF)
        s_ctx = jnp.einsum('bqhgd,bkhd->bhgqk', qb, k_ctx).astype(jnp.float32) * scale
        p = jax.nn.softmax(jnp.concatenate([s_lat, s_ctx, sink_col], axis=-1), axis=-1).astype(v.dtype)
        p_lat = p[..., :3 * Q_BLOCK]
        p_ctx = p[..., 3 * Q_BLOCK:3 * Q_BLOCK + C]
        return (jnp.einsum('bhgqk,bkhd->bqhgd', p_lat, vb)
                + jnp.einsum('bhgqk,bkhd->bqhgd', p_ctx, v_ctx))

    o = lax.map(block, jnp.arange(nblk))
    return jnp.moveaxis(o, 0, 1).reshape(B, L, N_KV_HEADS, GROUP, HEAD_DIM)


def context_attention(q, k, v, sink):
    B, C = q.shape[:2]
    s = jnp.einsum('bqhgd,bkhd->bhgqk', q, k).astype(jnp.float32) * (HEAD_DIM ** -0.5)
    sink_col = jnp.broadcast_to(sink.astype(jnp.float32).reshape(1, N_KV_HEADS, GROUP, 1, 1),
                                (B, N_KV_HEADS, GROUP, C, 1))
    p = jax.nn.softmax(jnp.concatenate([s, sink_col], axis=-1), axis=-1).astype(v.dtype)
    return jnp.einsum('bhgqk,bkhd->bqhgd', p[..., :C], v)


def attend_latent(q, k, v, k_ctx, v_ctx, sink, cos_r, sin_r, cos_c, sin_c):
    B, L, _ = q.shape
    q = apply_axial_rope(q.reshape(B, L, N_KV_HEADS, GROUP, HEAD_DIM), cos_r, sin_r, cos_c, sin_c)
    k = apply_axial_rope(k.reshape(B, L, N_KV_HEADS, HEAD_DIM), cos_r, sin_r, cos_c, sin_c)
    v = v.reshape(B, L, N_KV_HEADS, HEAD_DIM)
    o = windowed_attention(q, k, v, k_ctx, v_ctx, sink)
    return o.reshape(B, L, ATTN_WIDTH)


def attend_context(q, k, v, sink):
    B, C, _ = q.shape
    o = context_attention(q.reshape(B, C, N_KV_HEADS, GROUP, HEAD_DIM),
                          k.reshape(B, C, N_KV_HEADS, HEAD_DIM),
                          v.reshape(B, C, N_KV_HEADS, HEAD_DIM), sink)
    return o.reshape(B, C, ATTN_WIDTH)


def mixer_sublayer(h, w_in, pool_w, pool_scale, w_branch_a, w_branch_b, w_out, attend):
    xa, ga, q, k, v, gb, gm = split_columns(h @ w_in)
    y_a = pool_mixer(xa, pool_w, pool_scale) * jax.nn.silu(ga)
    y_b = attend(q, k, v) * jax.nn.silu(gb)
    g = jax.nn.sigmoid(gm)
    merged = g[..., :D_MODEL] * (y_a @ w_branch_a) + g[..., D_MODEL:] * (y_b @ w_branch_b)
    return merged @ w_out


def setup_inputs(seed: int = 0) -> dict:
    key = jax.random.key(seed)
    ks = jax.random.split(key, 16)
    f32 = jnp.float32
    nrm = lambda k, shp, s: jax.random.normal(k, shp, f32) * s
    return {
        "x": nrm(ks[0], (BATCH, SEQ, D_MODEL), 1.0),
        "c": nrm(ks[1], (BATCH, D_MODEL), 1.0),
        "ctx": nrm(ks[2], (BATCH, CTX_LEN, D_MODEL), 1.0),
        "c_ctx": nrm(ks[3], (D_MODEL,), 1.0),
        "w_mod": nrm(ks[4], (DEPTH, D_MODEL, 3 * D_MODEL), 0.5 * D_MODEL ** -0.5),
        "b_mod": nrm(ks[5], (DEPTH, 3 * D_MODEL), 0.02),
        "norm_pre_g": 1.0 + nrm(ks[6], (DEPTH, D_MODEL), 0.05),
        "norm_post_g": 1.0 + nrm(ks[7], (DEPTH, D_MODEL), 0.05),
        "w_in": nrm(ks[8], (DEPTH, D_MODEL, IN_WIDTH), D_MODEL ** -0.5),
        "pool_w": nrm(ks[9], (DEPTH, POOL_GROUPS, POOL_GC, POOL_GC), POOL_GC ** -0.5),
        "pool_scale": 0.5 + nrm(ks[10], (DEPTH, POOL_WIDTH), 0.1),
        "sink": nrm(ks[11], (DEPTH, N_HEADS), 0.5),
        "w_branch_a": nrm(ks[12], (DEPTH, POOL_WIDTH, D_MODEL), POOL_WIDTH ** -0.5),
        "w_branch_b": nrm(ks[13], (DEPTH, ATTN_WIDTH, D_MODEL), ATTN_WIDTH ** -0.5),
        "w_out": nrm(ks[14], (DEPTH, D_MODEL, D_MODEL), D_MODEL ** -0.5),
    }


def reference(x, c, ctx, c_ctx, w_mod, b_mod, norm_pre_g, norm_post_g, w_in, pool_w, pool_scale,
              sink, w_branch_a, w_branch_b, w_out):
    B, L, _ = x.shape
    C = ctx.shape[1]
    ROWS = L // GRID_W
    row = jnp.repeat(jnp.arange(ROWS), GRID_W)
    col = jnp.tile(jnp.arange(GRID_W), ROWS)
    cos_r, sin_r = rope_tables(row)
    cos_c, sin_c = rope_tables(col)

    for i in range(DEPTH):
        shift, scale, gate = adaln(c, w_mod[i], b_mod[i])
        c_shift, c_scale, c_gate = adaln(c_ctx, w_mod[i], b_mod[i])

        h_ctx = modulate(rms_norm(ctx, norm_pre_g[i]), c_shift, c_scale)
        k_ctx = (h_ctx @ w_in[i][:, K_OFF:K_OFF + KV_WIDTH]).reshape(B, C, N_KV_HEADS, HEAD_DIM)
        v_ctx = (h_ctx @ w_in[i][:, V_OFF:V_OFF + KV_WIDTH]).reshape(B, C, N_KV_HEADS, HEAD_DIM)

        h = modulate(rms_norm(x, norm_pre_g[i]), shift[:, None, :], scale[:, None, :])
        attend = functools.partial(attend_latent, k_ctx=k_ctx, v_ctx=v_ctx, sink=sink[i],
                                   cos_r=cos_r, sin_r=sin_r, cos_c=cos_c, sin_c=sin_c)
        out = mixer_sublayer(h, w_in[i], pool_w[i], pool_scale[i], w_branch_a[i], w_branch_b[i],
                             w_out[i], attend)
        x_new = x + gate[:, None, :] * rms_norm(out, norm_post_g[i])

        if i < DEPTH - 1:
            attend_c = functools.partial(attend_context, sink=sink[i])
            out_c = mixer_sublayer(h_ctx, w_in[i], pool_w[i], pool_scale[i], w_branch_a[i],
                                   w_branch_b[i], w_out[i], attend_c)
            ctx = ctx + c_gate * rms_norm(out_c, norm_post_g[i])
        x = x_new
    return x
```

```python
import functools

import jax
import jax.numpy as jnp
from jax import lax
from jax.experimental import pallas as pl
from jax.experimental.pallas import tpu as pltpu

F32 = jnp.float32
BF16 = jnp.bfloat16

D_MODEL = 1024
GRID_W = 64
POOL_WIDTH = 512
POOL_WINDOWS = (2, 4, 8, 16)
POOL_GC = POOL_WIDTH // len(POOL_WINDOWS)
N_HEADS = 16
N_KV_HEADS = 4
GROUP = N_HEADS // N_KV_HEADS
HEAD_DIM = 64
ATTN_WIDTH = N_HEADS * HEAD_DIM
KV_WIDTH = N_KV_HEADS * HEAD_DIM
Q_BLOCK = 128
ROPE_BASE = 10000.0
ROPE_HALF = 16
EPS = 1e-6
NEG_INF = -1e30
POOL_HALO = 16

_XA, _GA, _Q, _K, _V, _GB, _GM = 0, 512, 1024, 2048, 2304, 2560, 3584
IN_WIDTH = 5632
_T_XA, _T_GA, _T_K, _T_GB, _T_GM, _T_END = 0, 512, 1024, 1280, 2304, 4352

VMEM_LIMIT_BYTES = 48 * 1024 * 1024


def _const_spec(shape):
    nd = len(shape)
    return pl.BlockSpec(shape, lambda *_: (0,) * nd, pipeline_mode=pl.Buffered(1))


def _silu(v):
    return v * jax.nn.sigmoid(v)


def _adaln_kernel(c_ref, w_ref, b_ref, o_ref):
    o_ref[...] = jnp.dot(_silu(c_ref[...]), w_ref[...], preferred_element_type=F32) + b_ref[...]


def _adaln(c_all, w_mod, b_mod):
    rows = c_all.shape[0]
    n = w_mod.shape[1]
    bn = D_MODEL
    return pl.pallas_call(
        _adaln_kernel,
        grid=(n // bn,),
        in_specs=[pl.BlockSpec((rows, D_MODEL), lambda j: (0, 0)),
                  pl.BlockSpec((D_MODEL, bn), lambda j: (0, j)),
                  pl.BlockSpec((1, bn), lambda j: (0, j))],
        out_specs=pl.BlockSpec((rows, bn), lambda j: (0, j)),
        out_shape=jax.ShapeDtypeStruct((rows, n), F32),
        compiler_params=pltpu.CompilerParams(dimension_semantics=("arbitrary",),
                                             vmem_limit_bytes=VMEM_LIMIT_BYTES),
        name="adaln",
    )(c_all, w_mod, b_mod)


def _norm_modulate(x, g, shift, scale):
    ms = jnp.mean(x * x, axis=-1, keepdims=True)
    y = x * lax.rsqrt(ms + EPS) * g
    return y * (1.0 + scale) + shift


def _dot_t(w_t, h):
    return lax.dot_general(w_t, h, (((1,), (1,)), ((), ())), preferred_element_type=F32)


def _ctx_kv_kernel(ctx_ref, g_ref, shift_ref, scale_ref, wk_ref, wvt_ref, k_ref, vt_ref):
    h = _norm_modulate(ctx_ref[0], g_ref[...], shift_ref[...], scale_ref[...]).astype(BF16)
    k_ref[0] = jnp.dot(h, wk_ref[...], preferred_element_type=F32).astype(BF16)
    vt_ref[0] = _dot_t(wvt_ref[...], h).astype(BF16)


def _ctx_kv(ctx, g, shift, scale, wk, wvt):
    B, C, _ = ctx.shape
    return pl.pallas_call(
        _ctx_kv_kernel,
        grid=(B,),
        in_specs=[pl.BlockSpec((1, C, D_MODEL), lambda b: (b, 0, 0)),
                  _const_spec((1, D_MODEL)), _const_spec((1, D_MODEL)), _const_spec((1, D_MODEL)),
                  _const_spec((D_MODEL, KV_WIDTH)), _const_spec((KV_WIDTH, D_MODEL))],
        out_specs=[pl.BlockSpec((1, C, KV_WIDTH), lambda b: (b, 0, 0)),
                   pl.BlockSpec((1, KV_WIDTH, C), lambda b: (b, 0, 0))],
        out_shape=[jax.ShapeDtypeStruct((B, C, KV_WIDTH), BF16),
                   jax.ShapeDtypeStruct((B, KV_WIDTH, C), BF16)],
        compiler_params=pltpu.CompilerParams(dimension_semantics=("arbitrary",),
                                             vmem_limit_bytes=VMEM_LIMIT_BYTES),
        name="ctx_kv",
    )(ctx, g, shift, scale, wk, wvt)


def _proj_kernel(x_ref, shift_ref, scale_ref, g_ref, wtok_ref, wft_ref,
                 cosk_ref, sink_ref, cosq_ref, sinq_ref,
                 xa_ref, ga_ref, k_ref, gb_ref, gm_ref, qt_ref, vt_ref):
    tm = x_ref.shape[1]
    h = _norm_modulate(x_ref[0], g_ref[...], shift_ref[0], scale_ref[0]).astype(BF16)

    def tok(lo, hi):
        return jnp.dot(h, wtok_ref[:, lo:hi], preferred_element_type=F32)

    xa_ref[0] = tok(_T_XA, _T_GA).astype(BF16)
    ga_ref[0] = tok(_T_GA, _T_K).astype(BF16)
    gb_ref[0] = tok(_T_GB, _T_GM).astype(BF16)
    gm_ref[0] = tok(_T_GM, _T_END).astype(BF16)

    kf = tok(_T_K, _T_GB)
    lane = lax.broadcasted_iota(jnp.int32, kf.shape, 1)
    partner = jnp.where((lane & (2 * ROPE_HALF - 1)) < ROPE_HALF,
                        pltpu.roll(kf, KV_WIDTH - ROPE_HALF, 1),
                        pltpu.roll(kf, ROPE_HALF, 1))
    k_ref[0] = (kf * cosk_ref[...] + partner * sink_ref[...]).astype(BF16)

    qt = _dot_t(wft_ref[0:ATTN_WIDTH, :], h)
    q4 = qt.reshape(ATTN_WIDTH // (2 * ROPE_HALF), 2, ROPE_HALF, tm)
    qp = jnp.concatenate([q4[:, 1:2], q4[:, 0:1]], axis=1).reshape(ATTN_WIDTH, tm)
    cosq = jnp.tile(cosq_ref[...], (N_HEADS, 1))
    sinq = jnp.tile(sinq_ref[...], (N_HEADS, 1))
    qt_ref[0] = (qt * cosq + qp * sinq).astype(BF16)

    vt_ref[0] = _dot_t(wft_ref[ATTN_WIDTH:ATTN_WIDTH + KV_WIDTH, :], h).astype(BF16)


def _proj(x, shift, scale, g, wtok, wft, cosk, sink, cosq, sinq, tm):
    B, L, _ = x.shape
    nt = L // tm
    tok_map = lambda t, b: (b, t, 0)
    feat_map = lambda t, b: (b, 0, t)
    vec_map = lambda t, b: (b, 0, 0)
    return pl.pallas_call(
        _proj_kernel,
        grid=(nt, B),
        in_specs=[pl.BlockSpec((1, tm, D_MODEL), tok_map),
                  pl.BlockSpec((1, 1, D_MODEL), vec_map),
                  pl.BlockSpec((1, 1, D_MODEL), vec_map),
                  _const_spec((1, D_MODEL)),
                  _const_spec((D_MODEL, _T_END)),
                  _const_spec((ATTN_WIDTH + KV_WIDTH, D_MODEL)),
                  pl.BlockSpec((tm, KV_WIDTH), lambda t, b: (t, 0)),
                  pl.BlockSpec((tm, KV_WIDTH), lambda t, b: (t, 0)),
                  pl.BlockSpec((HEAD_DIM, tm), lambda t, b: (0, t)),
                  pl.BlockSpec((HEAD_DIM, tm), lambda t, b: (0, t))],
        out_specs=[pl.BlockSpec((1, tm, POOL_WIDTH), tok_map),
                   pl.BlockSpec((1, tm, POOL_WIDTH), tok_map),
                   pl.BlockSpec((1, tm, KV_WIDTH), tok_map),
                   pl.BlockSpec((1, tm, ATTN_WIDTH), tok_map),
                   pl.BlockSpec((1, tm, 2 * D_MODEL), tok_map),
                   pl.BlockSpec((1, ATTN_WIDTH, tm), feat_map),
                   pl.BlockSpec((1, KV_WIDTH, tm), feat_map)],
        out_shape=[jax.ShapeDtypeStruct((B, L, POOL_WIDTH), BF16),
                   jax.ShapeDtypeStruct((B, L, POOL_WIDTH), BF16),
                   jax.ShapeDtypeStruct((B, L, KV_WIDTH), BF16),
                   jax.ShapeDtypeStruct((B, L, ATTN_WIDTH), BF16),
                   jax.ShapeDtypeStruct((B, L, 2 * D_MODEL), BF16),
                   jax.ShapeDtypeStruct((B, ATTN_WIDTH, L), BF16),
                   jax.ShapeDtypeStruct((B, KV_WIDTH, L), BF16)],
        compiler_params=pltpu.CompilerParams(dimension_semantics=("arbitrary", "arbitrary"),
                                             vmem_limit_bytes=VMEM_LIMIT_BYTES),
        name="proj",
    )(x, shift, scale, g, wtok, wft, cosk, sink, cosq, sinq)


def _attn_kernel(qt_ref, kp_ref, kc_ref, kn_ref, vp_ref, vc_ref, vn_ref, kx_ref, vx_ref, sink_ref,
                 o_ref, ot_scr):
    i = pl.program_id(1)
    nblk = pl.num_programs(1)
    nq = GROUP * Q_BLOCK

    kk = lax.broadcasted_iota(jnp.int32, (Q_BLOCK, nq), 0)
    qq = lax.broadcasted_iota(jnp.int32, (Q_BLOCK, nq), 1) & (Q_BLOCK - 1)
    mask_p = (kk >= qq) & (i > 0)
    mask_n = (kk <= qq) & (i < nblk - 1)

    kp, kc, kn, kx = kp_ref[0], kc_ref[0], kn_ref[0], kx_ref[0]
    vt_all = jnp.concatenate([vp_ref[0], vc_ref[0], vn_ref[0], vx_ref[0]], axis=1)

    for h in range(N_KV_HEADS):
        q_h = jnp.concatenate(
            [qt_ref[0, (GROUP * h + g) * HEAD_DIM:(GROUP * h + g + 1) * HEAD_DIM, :] for g in range(GROUP)],
            axis=1)
        pieces = []
        if h > 0:
            pieces.append(jnp.zeros((h * HEAD_DIM, nq), BF16))
        pieces.append(q_h)
        if h < N_KV_HEADS - 1:
            pieces.append(jnp.zeros(((N_KV_HEADS - 1 - h) * HEAD_DIM, nq), BF16))
        q_pad = jnp.concatenate(pieces, axis=0)

        s_p = jnp.where(mask_p, jnp.dot(kp, q_pad, preferred_element_type=F32), NEG_INF)
        s_c = jnp.dot(kc, q_pad, preferred_element_type=F32)
        s_n = jnp.where(mask_n, jnp.dot(kn, q_pad, preferred_element_type=F32), NEG_INF)
        s_x = jnp.dot(kx, q_pad, preferred_element_type=F32)
        sink = sink_ref[h]

        m = jnp.maximum(
            jnp.maximum(jnp.max(s_p, axis=0, keepdims=True), jnp.max(s_c, axis=0, keepdims=True)),
            jnp.maximum(jnp.max(s_n, axis=0, keepdims=True), jnp.max(s_x, axis=0, keepdims=True)))
        m = jnp.maximum(m, sink)
        p_p, p_c, p_n, p_x = (jnp.exp(s - m) for s in (s_p, s_c, s_n, s_x))
        denom = (jnp.sum(p_p, axis=0, keepdims=True) + jnp.sum(p_c, axis=0, keepdims=True)
                 + jnp.sum(p_n, axis=0, keepdims=True) + jnp.sum(p_x, axis=0, keepdims=True)
                 + jnp.exp(sink - m))
        p_all = jnp.concatenate([p_p, p_c, p_n, p_x], axis=0).astype(BF16)
        o_t = jnp.dot(vt_all[h * HEAD_DIM:(h + 1) * HEAD_DIM, :], p_all,
                      preferred_element_type=F32)
        o_t = o_t * (1.0 / denom)
        for g in range(GROUP):
            r = (GROUP * h + g) * HEAD_DIM
            ot_scr[r:r + HEAD_DIM, :] = o_t[:, g * Q_BLOCK:(g + 1) * Q_BLOCK]

    o_ref[0] = ot_scr[...].T.astype(BF16)


def _attn(qt, k, vt, kx, vxt, sink_rows):
    B, L, _ = k.shape
    C = kx.shape[1]
    nblk = L // Q_BLOCK
    prev = lambda i: jnp.maximum(i - 1, 0)
    nxt = lambda i: jnp.minimum(i + 1, nblk - 1)
    kspec = lambda f: pl.BlockSpec((1, Q_BLOCK, KV_WIDTH), lambda b, i: (b, f(i), 0))
    vspec = lambda f: pl.BlockSpec((1, KV_WIDTH, Q_BLOCK), lambda b, i: (b, 0, f(i)))
    same = lambda i: i
    return pl.pallas_call(
        _attn_kernel,
        grid=(B, nblk),
        in_specs=[pl.BlockSpec((1, ATTN_WIDTH, Q_BLOCK), lambda b, i: (b, 0, i)),
                  kspec(prev), kspec(same), kspec(nxt),
                  vspec(prev), vspec(same), vspec(nxt),
                  pl.BlockSpec((1, C, KV_WIDTH), lambda b, i: (b, 0, 0)),
                  pl.BlockSpec((1, KV_WIDTH, C), lambda b, i: (b, 0, 0)),
                  _const_spec((N_KV_HEADS, 1, GROUP * Q_BLOCK))],
        out_specs=pl.BlockSpec((1, Q_BLOCK, ATTN_WIDTH), lambda b, i: (b, i, 0)),
        out_shape=jax.ShapeDtypeStruct((B, L, ATTN_WIDTH), BF16),
        scratch_shapes=[pltpu.VMEM((ATTN_WIDTH, Q_BLOCK), F32)],
        compiler_params=pltpu.CompilerParams(dimension_semantics=("arbitrary", "arbitrary"),
                                             vmem_limit_bytes=VMEM_LIMIT_BYTES),
        name="attn",
    )(qt, k, k, k, vt, vt, vt, kx, vxt, sink_rows)


def _tail_kernel(x_ref, gate_ref, xa_ref, xap_ref, xan_ref, ga_ref, o_ref, gb_ref, gm_ref,
                 pw_ref, ps_ref, wa_ref, wb_ref, wo_ref, gpost_ref, out_ref, ext_scr, *, seq_len):
    t = pl.program_id(1)
    nt = pl.num_programs(1)
    tm = x_ref.shape[1]

    ext_scr[0:POOL_HALO, :] = jnp.where(t > 0, xap_ref[0].astype(F32), 0.0)
    ext_scr[POOL_HALO:POOL_HALO + tm, :] = xa_ref[0].astype(F32)
    ext_scr[POOL_HALO + tm:, :] = jnp.where(t < nt - 1, xan_ref[0].astype(F32), 0.0)
    pos = t * tm + lax.broadcasted_iota(jnp.int32, (tm, 1), 0)
    ys = []
    for gi, w in enumerate(POOL_WINDOWS):
        cols = slice(gi * POOL_GC, (gi + 1) * POOL_GC)
        acc = ext_scr[POOL_HALO - w // 2:POOL_HALO - w // 2 + tm, cols]
        for j in range(-w // 2 + 1, w // 2):
            acc = acc + ext_scr[POOL_HALO + j:POOL_HALO + j + tm, cols]
        cnt = (jnp.minimum(pos + w // 2, seq_len) - jnp.maximum(pos - w // 2, 0)).astype(F32)
        pooled = acc * (1.0 / cnt) - ext_scr[POOL_HALO:POOL_HALO + tm, cols]
        ys.append(jnp.dot(pooled.astype(BF16), pw_ref[gi], preferred_element_type=F32))
    y_a = jnp.concatenate(ys, axis=1) * ps_ref[...] * _silu(ga_ref[0].astype(F32))
    z_a = jnp.dot(y_a.astype(BF16), wa_ref[...], preferred_element_type=F32)

    y_b = o_ref[0].astype(F32) * _silu(gb_ref[0].astype(F32))
    z_b = jnp.dot(y_b.astype(BF16), wb_ref[...], preferred_element_type=F32)

    gsig = jax.nn.sigmoid(gm_ref[0].astype(F32))
    merged = gsig[:, :D_MODEL] * z_a + gsig[:, D_MODEL:] * z_b
    out = jnp.dot(merged.astype(BF16), wo_ref[...], preferred_element_type=F32)

    ms = jnp.mean(out * out, axis=-1, keepdims=True)
    normed = out * lax.rsqrt(ms + EPS) * gpost_ref[...]
    out_ref[0] = x_ref[0] + gate_ref[0] * normed


def _tail(x, gate, xa, ga, o, gb, gm, pool_w, pool_scale, wa, wb, wo, gpost, tm):
    B, L, _ = x.shape
    nt = L // tm
    hb = tm // POOL_HALO
    nhalo = L // POOL_HALO
    tok_map = lambda b, t: (b, t, 0)
    return pl.pallas_call(
        functools.partial(_tail_kernel, seq_len=L),
        grid=(B, nt),
        in_specs=[pl.BlockSpec((1, tm, D_MODEL), tok_map),
                  pl.BlockSpec((1, 1, D_MODEL), lambda b, t: (b, 0, 0)),
                  pl.BlockSpec((1, tm, POOL_WIDTH), tok_map),
                  pl.BlockSpec((1, POOL_HALO, POOL_WIDTH),
                               lambda b, t: (b, jnp.maximum(t * hb - 1, 0), 0)),
                  pl.BlockSpec((1, POOL_HALO, POOL_WIDTH),
                               lambda b, t: (b, jnp.minimum((t + 1) * hb, nhalo - 1), 0)),
                  pl.BlockSpec((1, tm, POOL_WIDTH), tok_map),
                  pl.BlockSpec((1, tm, ATTN_WIDTH), tok_map),
                  pl.BlockSpec((1, tm, ATTN_WIDTH), tok_map),
                  pl.BlockSpec((1, tm, 2 * D_MODEL), tok_map),
                  _const_spec((len(POOL_WINDOWS), POOL_GC, POOL_GC)),
                  _const_spec((1, POOL_WIDTH)),
                  _const_spec((POOL_WIDTH, D_MODEL)),
                  _const_spec((ATTN_WIDTH, D_MODEL)),
                  _const_spec((D_MODEL, D_MODEL)),
                  _const_spec((1, D_MODEL))],
        out_specs=pl.BlockSpec((1, tm, D_MODEL), tok_map),
        out_shape=jax.ShapeDtypeStruct((B, L, D_MODEL), F32),
        scratch_shapes=[pltpu.VMEM((tm + 2 * POOL_HALO, POOL_WIDTH), F32)],
        compiler_params=pltpu.CompilerParams(dimension_semantics=("arbitrary", "arbitrary"),
                                             vmem_limit_bytes=VMEM_LIMIT_BYTES),
        name="tail",
    )(x, gate, xa, xa, xa, ga, o, gb, gm, pool_w, pool_scale, wa, wb, wo, gpost)


def _rope_tables(L):
    rows = L // GRID_W
    row = jnp.repeat(jnp.arange(rows), GRID_W).astype(F32)
    col = jnp.tile(jnp.arange(GRID_W), rows).astype(F32)
    freqs = ROPE_BASE ** (-jnp.arange(ROPE_HALF, dtype=F32) / ROPE_HALF)
    ang_r = row[:, None] * freqs[None, :]
    ang_c = col[:, None] * freqs[None, :]
    cos = jnp.concatenate([jnp.cos(ang_r), jnp.cos(ang_r), jnp.cos(ang_c), jnp.cos(ang_c)], axis=1)
    sin = jnp.concatenate([-jnp.sin(ang_r), jnp.sin(ang_r), -jnp.sin(ang_c), jnp.sin(ang_c)], axis=1)
    return cos, sin


def kernel(x, c, ctx, c_ctx, w_mod, b_mod, norm_pre_g, norm_post_g, w_in, pool_w, pool_scale, sink,
           w_branch_a, w_branch_b, w_out):
    B, L, D = x.shape
    depth = w_in.shape[0]
    assert D == D_MODEL and L % Q_BLOCK == 0 and L % GRID_W == 0 and w_in.shape[2] == IN_WIDTH
    assert depth == 1, "context-stream update between layers is not implemented"
    tm = min(256, L)

    cos, sin = _rope_tables(L)
    cosk = jnp.tile(cos, (1, N_KV_HEADS))
    sink_k = jnp.tile(sin, (1, N_KV_HEADS))
    qscale = HEAD_DIM ** -0.5
    cosq = (cos * qscale).T
    sinq = (sin * qscale).T

    for i in range(depth):
        pad = (-(B + 1)) % 8
        c_all = jnp.concatenate([c, c_ctx[None, :], jnp.zeros((pad, D), F32)], axis=0)
        mod = _adaln(c_all, w_mod[i], b_mod[i][None, :])
        shift, scale, gate = (mod[:B, j * D:(j + 1) * D][:, None, :] for j in range(3))
        c_shift, c_scale = (mod[B:B + 1, j * D:(j + 1) * D] for j in range(2))

        w = w_in[i]
        wtok = jnp.concatenate([w[:, _XA:_Q], w[:, _K:_V], w[:, _GB:]], axis=1).astype(BF16)
        wft = jnp.concatenate([w[:, _Q:_K], w[:, _V:_GB]], axis=1).T.astype(BF16)
        g_pre = norm_pre_g[i][None, :]

        kx, vxt = _ctx_kv(ctx, g_pre, c_shift, c_scale,
                          w[:, _K:_V].astype(BF16), w[:, _V:_GB].T.astype(BF16))
        xa, ga, k, gb, gm, qt, vt = _proj(x, shift, scale, g_pre, wtok, wft, cosk, sink_k, cosq, sinq, tm)

        sink_rows = jnp.repeat(sink[i].astype(F32), Q_BLOCK).reshape(N_KV_HEADS, 1, GROUP * Q_BLOCK)
        o = _attn(qt, k, vt, kx, vxt, sink_rows)

        x = _tail(x, gate, xa, ga, o, gb, gm,
                  pool_w[i].astype(BF16), pool_scale[i][None, :],
                  w_branch_a[i].astype(BF16), w_branch_b[i].astype(BF16), w_out[i].astype(BF16),
                  norm_post_g[i][None, :], tm)
    return x
```

```python
import functools

import jax
import jax.numpy as jnp
from jax import lax
from jax.experimental import pallas as pl
from jax.experimental.pallas import tpu as pltpu

F32 = jnp.float32
BF16 = jnp.bfloat16

D_MODEL = 1024
GRID_W = 64
POOL_WIDTH = 512
POOL_WINDOWS = (2, 4, 8, 16)
POOL_GC = POOL_WIDTH // len(POOL_WINDOWS)
N_HEADS = 16
N_KV_HEADS = 4
GROUP = N_HEADS // N_KV_HEADS
HEAD_DIM = 64
ATTN_WIDTH = N_HEADS * HEAD_DIM
KV_WIDTH = N_KV_HEADS * HEAD_DIM
Q_BLOCK = 128
ROPE_BASE = 10000.0
ROPE_HALF = 16
EPS = 1e-6
NEG_INF = -1e30
BF16_SUBLANES = 16
POOL_HALO = BF16_SUBLANES
LOG2_E = 1.4426950408889634
ATTN_BLOCKS_PER_STEP = 4

_XA, _GA, _Q, _K, _V, _GB, _GM = 0, 512, 1024, 2048, 2304, 2560, 3584
IN_WIDTH = 5632
_T_XA, _T_GA, _T_K, _T_GB, _T_GM, _T_END = 0, 512, 1024, 1280, 2304, 4352

VMEM_LIMIT_BYTES = 48 * 1024 * 1024


def _const_spec(shape):
    nd = len(shape)
    return pl.BlockSpec(shape, lambda *_: (0,) * nd, pipeline_mode=pl.Buffered(1))


def _silu(v):
    return v * jax.nn.sigmoid(v)


def _adaln_kernel(c_ref, w_ref, b_ref, o_ref):
    o_ref[...] = jnp.dot(_silu(c_ref[...]), w_ref[...], preferred_element_type=F32) + b_ref[...]


def _adaln(c_all, w_mod, b_mod):
    rows = c_all.shape[0]
    n = w_mod.shape[1]
    bn = D_MODEL
    return pl.pallas_call(
        _adaln_kernel,
        grid=(n // bn,),
        in_specs=[pl.BlockSpec((rows, D_MODEL), lambda j: (0, 0)),
                  pl.BlockSpec((D_MODEL, bn), lambda j: (0, j)),
                  pl.BlockSpec((1, bn), lambda j: (0, j))],
        out_specs=pl.BlockSpec((rows, bn), lambda j: (0, j)),
        out_shape=jax.ShapeDtypeStruct((rows, n), F32),
        compiler_params=pltpu.CompilerParams(dimension_semantics=("arbitrary",),
                                             vmem_limit_bytes=VMEM_LIMIT_BYTES),
        name="adaln",
    )(c_all, w_mod, b_mod)


def _norm_modulate(x, g, shift, scale):
    ms = jnp.mean(x * x, axis=-1, keepdims=True)
    y = x * lax.rsqrt(ms + EPS) * g
    return y * (1.0 + scale) + shift


def _dot_t(w_t, h):
    return lax.dot_general(w_t, h, (((1,), (1,)), ((), ())), preferred_element_type=F32)


def _ctx_kv_kernel(ctx_ref, g_ref, shift_ref, scale_ref, wk_ref, wvt_ref, k_ref, vt_ref):
    h = _norm_modulate(ctx_ref[0], g_ref[...], shift_ref[...], scale_ref[...]).astype(BF16)
    k_ref[0] = jnp.dot(h, wk_ref[...], preferred_element_type=F32).astype(BF16)
    vt_ref[0] = _dot_t(wvt_ref[...], h).astype(BF16)


def _ctx_kv(ctx, g, shift, scale, wk, wvt):
    B, C, _ = ctx.shape
    return pl.pallas_call(
        _ctx_kv_kernel,
        grid=(B,),
        in_specs=[pl.BlockSpec((1, C, D_MODEL), lambda b: (b, 0, 0)),
                  _const_spec((1, D_MODEL)), _const_spec((1, D_MODEL)), _const_spec((1, D_MODEL)),
                  _const_spec((D_MODEL, KV_WIDTH)), _const_spec((KV_WIDTH, D_MODEL))],
        out_specs=[pl.BlockSpec((1, C, KV_WIDTH), lambda b: (b, 0, 0)),
                   pl.BlockSpec((1, KV_WIDTH, C), lambda b: (b, 0, 0))],
        out_shape=[jax.ShapeDtypeStruct((B, C, KV_WIDTH), BF16),
                   jax.ShapeDtypeStruct((B, KV_WIDTH, C), BF16)],
        compiler_params=pltpu.CompilerParams(dimension_semantics=("arbitrary",),
                                             vmem_limit_bytes=VMEM_LIMIT_BYTES),
        name="ctx_kv",
    )(ctx, g, shift, scale, wk, wvt)


def _proj_kernel(x_ref, shift_ref, scale_ref, g_ref, wtok_ref, wft_ref,
                 cosk_ref, sink_ref, cosq_ref, sinq_ref,
                 xa_ref, ga_ref, k_ref, gb_ref, gm_ref, qt_ref, vt_ref):
    tm = x_ref.shape[1]
    h = _norm_modulate(x_ref[0], g_ref[...], shift_ref[0], scale_ref[0]).astype(BF16)

    def tok(lo, hi):
        return jnp.dot(h, wtok_ref[:, lo:hi], preferred_element_type=F32)

    xa_ref[0] = tok(_T_XA, _T_GA).astype(BF16)
    ga_ref[0] = tok(_T_GA, _T_K).astype(BF16)
    gb_ref[0] = tok(_T_GB, _T_GM).astype(BF16)
    gm_ref[0] = tok(_T_GM, _T_END).astype(BF16)

    kf = tok(_T_K, _T_GB)
    lane = lax.broadcasted_iota(jnp.int32, kf.shape, 1)
    partner = jnp.where((lane & (2 * ROPE_HALF - 1)) < ROPE_HALF,
                        pltpu.roll(kf, KV_WIDTH - ROPE_HALF, 1),
                        pltpu.roll(kf, ROPE_HALF, 1))
    k_ref[0] = (kf * cosk_ref[...] + partner * sink_ref[...]).astype(BF16)

    qt = _dot_t(wft_ref[0:ATTN_WIDTH, :], h)
    q4 = qt.reshape(ATTN_WIDTH // (2 * ROPE_HALF), 2, ROPE_HALF, tm)
    qp = jnp.concatenate([q4[:, 1:2], q4[:, 0:1]], axis=1).reshape(ATTN_WIDTH, tm)
    cosq = jnp.tile(cosq_ref[...], (N_HEADS, 1))
    sinq = jnp.tile(sinq_ref[...], (N_HEADS, 1))
    qt_ref[0] = (qt * cosq + qp * sinq).astype(BF16)

    vt_ref[0] = _dot_t(wft_ref[ATTN_WIDTH:ATTN_WIDTH + KV_WIDTH, :], h).astype(BF16)


def _proj(x, shift, scale, g, wtok, wft, cosk, sink, cosq, sinq, tm):
    B, L, _ = x.shape
    nt = L // tm
    tok_map = lambda t, b: (b, t, 0)
    feat_map = lambda t, b: (b, 0, t)
    vec_map = lambda t, b: (b, 0, 0)
    return pl.pallas_call(
        _proj_kernel,
        grid=(nt, B),
        in_specs=[pl.BlockSpec((1, tm, D_MODEL), tok_map),
                  pl.BlockSpec((1, 1, D_MODEL), vec_map),
                  pl.BlockSpec((1, 1, D_MODEL), vec_map),
                  _const_spec((1, D_MODEL)),
                  _const_spec((D_MODEL, _T_END)),
                  _const_spec((ATTN_WIDTH + KV_WIDTH, D_MODEL)),
                  pl.BlockSpec((tm, KV_WIDTH), lambda t, b: (t, 0)),
                  pl.BlockSpec((tm, KV_WIDTH), lambda t, b: (t, 0)),
                  pl.BlockSpec((HEAD_DIM, tm), lambda t, b: (0, t)),
                  pl.BlockSpec((HEAD_DIM, tm), lambda t, b: (0, t))],
        out_specs=[pl.BlockSpec((1, tm, POOL_WIDTH), tok_map),
                   pl.BlockSpec((1, tm, POOL_WIDTH), tok_map),
                   pl.BlockSpec((1, tm, KV_WIDTH), tok_map),
                   pl.BlockSpec((1, tm, ATTN_WIDTH), tok_map),
                   pl.BlockSpec((1, tm, 2 * D_MODEL), tok_map),
                   pl.BlockSpec((1, ATTN_WIDTH, tm), feat_map),
                   pl.BlockSpec((1, KV_WIDTH, tm), feat_map)],
        out_shape=[jax.ShapeDtypeStruct((B, L, POOL_WIDTH), BF16),
                   jax.ShapeDtypeStruct((B, L, POOL_WIDTH), BF16),
                   jax.ShapeDtypeStruct((B, L, KV_WIDTH), BF16),
                   jax.ShapeDtypeStruct((B, L, ATTN_WIDTH), BF16),
                   jax.ShapeDtypeStruct((B, L, 2 * D_MODEL), BF16),
                   jax.ShapeDtypeStruct((B, ATTN_WIDTH, L), BF16),
                   jax.ShapeDtypeStruct((B, KV_WIDTH, L), BF16)],
        compiler_params=pltpu.CompilerParams(dimension_semantics=("arbitrary", "arbitrary"),
                                             vmem_limit_bytes=VMEM_LIMIT_BYTES),
        name="proj",
    )(x, shift, scale, g, wtok, wft, cosk, sink, cosq, sinq)


def _attn_kernel(qt_ref, kp_ref, kc_ref, kn_ref, vp_ref, vc_ref, vn_ref, kx_ref, vx_ref, sink_ref,
                 o_ref, ot_scr):
    j = pl.program_id(1)
    nstep = pl.num_programs(1)
    nq = GROUP * Q_BLOCK
    qb = kc_ref.shape[1] // Q_BLOCK
    nctx = kx_ref.shape[1]

    kk = lax.broadcasted_iota(jnp.int32, (Q_BLOCK, nq), 0)
    qq = lax.broadcasted_iota(jnp.int32, (Q_BLOCK, nq), 1) & (Q_BLOCK - 1)
    tri_p = kk >= qq
    tri_n = kk <= qq
    ones_rows = jnp.ones((BF16_SUBLANES, 3 * Q_BLOCK + nctx), BF16)
    kx, vx = kx_ref[0], vx_ref[0]

    def blk(ref_c, ref_edge, u, lane_major):
        if u < 0 or u >= qb:
            return ref_edge[0]
        if lane_major:
            return ref_c[0, :, u * Q_BLOCK:(u + 1) * Q_BLOCK]
        return ref_c[0, u * Q_BLOCK:(u + 1) * Q_BLOCK, :]

    def scores(u, h):
        q_h = jnp.concatenate(
            [qt_ref[0, (GROUP * h + g) * HEAD_DIM:(GROUP * h + g + 1) * HEAD_DIM,
                    u * Q_BLOCK:(u + 1) * Q_BLOCK] for g in range(GROUP)], axis=1)
        pieces = []
        if h > 0:
            pieces.append(jnp.zeros((h * HEAD_DIM, nq), BF16))
        pieces.append(q_h)
        if h < N_KV_HEADS - 1:
            pieces.append(jnp.zeros(((N_KV_HEADS - 1 - h) * HEAD_DIM, nq), BF16))
        q_pad = jnp.concatenate(pieces, axis=0)
        mask_p = tri_p & (j > 0) if u == 0 else tri_p
        mask_n = tri_n & (j < nstep - 1) if u == qb - 1 else tri_n
        kp = blk(kc_ref, kp_ref, u - 1, False)
        kc = blk(kc_ref, kp_ref, u, False)
        kn = blk(kc_ref, kn_ref, u + 1, False)
        s_p = jnp.where(mask_p, jnp.dot(kp, q_pad, preferred_element_type=F32), NEG_INF)
        s_c = jnp.dot(kc, q_pad, preferred_element_type=F32)
        s_n = jnp.where(mask_n, jnp.dot(kn, q_pad, preferred_element_type=F32), NEG_INF)
        s_x = jnp.dot(kx, q_pad, preferred_element_type=F32)
        return s_p, s_c, s_n, s_x

    def finish(u, h, s_all):
        s_p, s_c, s_n, s_x = s_all
        sink = sink_ref[h]
        m_blk = jnp.maximum(jnp.maximum(s_p, s_c), s_n)
        for r in range(0, nctx, Q_BLOCK):
            m_blk = jnp.maximum(m_blk, s_x[r:r + Q_BLOCK])
        m = jnp.maximum(jnp.max(m_blk, axis=0, keepdims=True), sink)
        p_all = jnp.concatenate([jnp.exp2(s - m) for s in (s_p, s_c, s_n, s_x)],
                                axis=0).astype(BF16)
        rows = slice(h * HEAD_DIM, (h + 1) * HEAD_DIM)
        v_aug = jnp.concatenate([blk(vc_ref, vp_ref, u - 1, True)[rows], blk(vc_ref, vp_ref, u, True)[rows],
                                 blk(vc_ref, vn_ref, u + 1, True)[rows], vx[rows]], axis=1)
        v_aug = jnp.concatenate([v_aug, ones_rows], axis=0)
        o_aug = jnp.dot(v_aug, p_all, preferred_element_type=F32)
        denom = o_aug[HEAD_DIM:HEAD_DIM + 1, :] + jnp.exp2(sink - m)
        o_t = o_aug[:HEAD_DIM, :] * (1.0 / denom)
        for g in range(GROUP):
            r = (GROUP * h + g) * HEAD_DIM
            ot_scr[r:r + HEAD_DIM, u * Q_BLOCK:(u + 1) * Q_BLOCK] = o_t[:, g * Q_BLOCK:(g + 1) * Q_BLOCK]

    units = [(u, h) for u in range(qb) for h in range(N_KV_HEADS)]
    ahead = 3
    pending = [scores(*un) for un in units[:ahead]]
    for n, un in enumerate(units):
        if n + ahead < len(units):
            pending.append(scores(*units[n + ahead]))
        finish(*un, pending.pop(0))
        if un[1] == N_KV_HEADS - 1:
            u = un[0]
            o_ref[0, u * Q_BLOCK:(u + 1) * Q_BLOCK, :] = (
                ot_scr[:, u * Q_BLOCK:(u + 1) * Q_BLOCK].T.astype(BF16))


def _attn(qt, k, vt, kx, vxt, sink_rows, qb):
    B, L, _ = k.shape
    C = kx.shape[1]
    nblk = L // Q_BLOCK
    nstep = nblk // qb
    tq = qb * Q_BLOCK
    prev = lambda j: jnp.maximum(j * qb - 1, 0)
    nxt = lambda j: jnp.minimum((j + 1) * qb, nblk - 1)
    return pl.pallas_call(
        _attn_kernel,
        grid=(B, nstep),
        in_specs=[pl.BlockSpec((1, ATTN_WIDTH, tq), lambda b, j: (b, 0, j)),
                  pl.BlockSpec((1, Q_BLOCK, KV_WIDTH), lambda b, j: (b, prev(j), 0)),
                  pl.BlockSpec((1, tq, KV_WIDTH), lambda b, j: (b, j, 0)),
                  pl.BlockSpec((1, Q_BLOCK, KV_WIDTH), lambda b, j: (b, nxt(j), 0)),
                  pl.BlockSpec((1, KV_WIDTH, Q_BLOCK), lambda b, j: (b, 0, prev(j))),
                  pl.BlockSpec((1, KV_WIDTH, tq), lambda b, j: (b, 0, j)),
                  pl.BlockSpec((1, KV_WIDTH, Q_BLOCK), lambda b, j: (b, 0, nxt(j))),
                  pl.BlockSpec((1, C, KV_WIDTH), lambda b, j: (b, 0, 0)),
                  pl.BlockSpec((1, KV_WIDTH, C), lambda b, j: (b, 0, 0)),
                  _const_spec((N_KV_HEADS, 1, GROUP * Q_BLOCK))],
        out_specs=pl.BlockSpec((1, tq, ATTN_WIDTH), lambda b, j: (b, j, 0)),
        out_shape=jax.ShapeDtypeStruct((B, L, ATTN_WIDTH), BF16),
        scratch_shapes=[pltpu.VMEM((ATTN_WIDTH, tq), F32)],
        compiler_params=pltpu.CompilerParams(dimension_semantics=("arbitrary", "arbitrary"),
                                             vmem_limit_bytes=VMEM_LIMIT_BYTES),
        name="attn",
    )(qt, k, k, k, vt, vt, vt, kx, vxt, sink_rows)


def _tail_kernel(x_ref, gate_ref, xa_ref, xap_ref, xan_ref, ga_ref, o_ref, gb_ref, gm_ref,
                 pw_ref, ps_ref, wa_ref, wb_ref, wo_ref, gpost_ref, out_ref, ext_scr, *, seq_len):
    t = pl.program_id(1)
    nt = pl.num_programs(1)
    tm = x_ref.shape[1]

    ext_scr[0:POOL_HALO, :] = jnp.where(t > 0, xap_ref[0].astype(F32), 0.0)
    ext_scr[POOL_HALO:POOL_HALO + tm, :] = xa_ref[0].astype(F32)
    ext_scr[POOL_HALO + tm:, :] = jnp.where(t < nt - 1, xan_ref[0].astype(F32), 0.0)
    pos = t * tm + lax.broadcasted_iota(jnp.int32, (tm, 1), 0)
    ys = []
    for gi, w in enumerate(POOL_WINDOWS):
        cols = slice(gi * POOL_GC, (gi + 1) * POOL_GC)
        acc = ext_scr[POOL_HALO - w // 2:POOL_HALO - w // 2 + tm, cols]
        for j in range(-w // 2 + 1, w // 2):
            acc = acc + ext_scr[POOL_HALO + j:POOL_HALO + j + tm, cols]
        cnt = (jnp.minimum(pos + w // 2, seq_len) - jnp.maximum(pos - w // 2, 0)).astype(F32)
        pooled = acc * (1.0 / cnt) - ext_scr[POOL_HALO:POOL_HALO + tm, cols]
        ys.append(jnp.dot(pooled.astype(BF16), pw_ref[gi], preferred_element_type=F32))
    y_a = jnp.concatenate(ys, axis=1) * ps_ref[...] * _silu(ga_ref[0].astype(F32))
    z_a = jnp.dot(y_a.astype(BF16), wa_ref[...], preferred_element_type=F32)

    y_b = o_ref[0].astype(F32) * _silu(gb_ref[0].astype(F32))
    z_b = jnp.dot(y_b.astype(BF16), wb_ref[...], preferred_element_type=F32)

    gsig = jax.nn.sigmoid(gm_ref[0].astype(F32))
    merged = gsig[:, :D_MODEL] * z_a + gsig[:, D_MODEL:] * z_b
    out = jnp.dot(merged.astype(BF16), wo_ref[...], preferred_element_type=F32)

    ms = jnp.mean(out * out, axis=-1, keepdims=True)
    normed = out * lax.rsqrt(ms + EPS) * gpost_ref[...]
    out_ref[0] = x_ref[0] + gate_ref[0] * normed


def _tail(x, gate, xa, ga, o, gb, gm, pool_w, pool_scale, wa, wb, wo, gpost, tm):
    B, L, _ = x.shape
    nt = L // tm
    hb = tm // POOL_HALO
    nhalo = L // POOL_HALO
    tok_map = lambda b, t: (b, t, 0)
    return pl.pallas_call(
        functools.partial(_tail_kernel, seq_len=L),
        grid=(B, nt),
        in_specs=[pl.BlockSpec((1, tm, D_MODEL), tok_map),
                  pl.BlockSpec((1, 1, D_MODEL), lambda b, t: (b, 0, 0)),
                  pl.BlockSpec((1, tm, POOL_WIDTH), tok_map),
                  pl.BlockSpec((1, POOL_HALO, POOL_WIDTH),
                               lambda b, t: (b, jnp.maximum(t * hb - 1, 0), 0)),
                  pl.BlockSpec((1, POOL_HALO, POOL_WIDTH),
                               lambda b, t: (b, jnp.minimum((t + 1) * hb, nhalo - 1), 0)),
                  pl.BlockSpec((1, tm, POOL_WIDTH), tok_map),
                  pl.BlockSpec((1, tm, ATTN_WIDTH), tok_map),
                  pl.BlockSpec((1, tm, ATTN_WIDTH), tok_map),
                  pl.BlockSpec((1, tm, 2 * D_MODEL), tok_map),
                  _const_spec((len(POOL_WINDOWS), POOL_GC, POOL_GC)),
                  _const_spec((1, POOL_WIDTH)),
                  _const_spec((POOL_WIDTH, D_MODEL)),
                  _const_spec((ATTN_WIDTH, D_MODEL)),
                  _const_spec((D_MODEL, D_MODEL)),
                  _const_spec((1, D_MODEL))],
        out_specs=pl.BlockSpec((1, tm, D_MODEL), tok_map),
        out_shape=jax.ShapeDtypeStruct((B, L, D_MODEL), F32),
        scratch_shapes=[pltpu.VMEM((tm + 2 * POOL_HALO, POOL_WIDTH), F32)],
        compiler_params=pltpu.CompilerParams(dimension_semantics=("arbitrary", "arbitrary"),
                                             vmem_limit_bytes=VMEM_LIMIT_BYTES),
        name="tail",
    )(x, gate, xa, xa, xa, ga, o, gb, gm, pool_w, pool_scale, wa, wb, wo, gpost)


def _rope_tables(L):
    rows = L // GRID_W
    row = jnp.repeat(jnp.arange(rows), GRID_W).astype(F32)
    col = jnp.tile(jnp.arange(GRID_W), rows).astype(F32)
    freqs = ROPE_BASE ** (-jnp.arange(ROPE_HALF, dtype=F32) / ROPE_HALF)
    ang_r = row[:, None] * freqs[None, :]
    ang_c = col[:, None] * freqs[None, :]
    cos = jnp.concatenate([jnp.cos(ang_r), jnp.cos(ang_r), jnp.cos(ang_c), jnp.cos(ang_c)], axis=1)
    sin = jnp.concatenate([-jnp.sin(ang_r), jnp.sin(ang_r), -jnp.sin(ang_c), jnp.sin(ang_c)], axis=1)
    return cos, sin


def kernel(x, c, ctx, c_ctx, w_mod, b_mod, norm_pre_g, norm_post_g, w_in, pool_w, pool_scale, sink,
           w_branch_a, w_branch_b, w_out):
    B, L, D = x.shape
    depth = w_in.shape[0]
    assert D == D_MODEL and L % Q_BLOCK == 0 and L % GRID_W == 0 and w_in.shape[2] == IN_WIDTH
    assert depth == 1, "context-stream update between layers is not implemented"
    tm = min(256, L)

    cos, sin = _rope_tables(L)
    cosk = jnp.tile(cos, (1, N_KV_HEADS))
    sink_k = jnp.tile(sin, (1, N_KV_HEADS))
    qscale = HEAD_DIM ** -0.5 * LOG2_E
    cosq = (cos * qscale).T
    sinq = (sin * qscale).T

    for i in range(depth):
        pad = (-(B + 1)) % 8
        c_all = jnp.concatenate([c, c_ctx[None, :], jnp.zeros((pad, D), F32)], axis=0)
        mod = _adaln(c_all, w_mod[i], b_mod[i][None, :])
        shift, scale, gate = (mod[:B, j * D:(j + 1) * D][:, None, :] for j in range(3))
        c_shift, c_scale = (mod[B:B + 1, j * D:(j + 1) * D] for j in range(2))

        w = w_in[i]
        wtok = jnp.concatenate([w[:, _XA:_Q], w[:, _K:_V], w[:, _GB:]], axis=1).astype(BF16)
        wft = jnp.concatenate([w[:, _Q:_K], w[:, _V:_GB]], axis=1).T.astype(BF16)
        g_pre = norm_pre_g[i][None, :]

        kx, vxt = _ctx_kv(ctx, g_pre, c_shift, c_scale,
                          w[:, _K:_V].astype(BF16), w[:, _V:_GB].T.astype(BF16))
        xa, ga, k, gb, gm, qt, vt = _proj(x, shift, scale, g_pre, wtok, wft, cosk, sink_k, cosq, sinq, tm)

        sink_rows = jnp.repeat(sink[i].astype(F32) * LOG2_E, Q_BLOCK).reshape(N_KV_HEADS, 1, GROUP * Q_BLOCK)
        o = _attn(qt, k, vt, kx, vxt, sink_rows, ATTN_BLOCKS_PER_STEP)

        x = _tail(x, gate, xa, ga, o, gb, gm,
                  pool_w[i].astype(BF16), pool_scale[i][None, :],
                  w_branch_a[i].astype(BF16), w_branch_b[i].astype(BF16), w_out[i].astype(BF16),
                  norm_post_g[i][None, :], tm)
    return x
```

```python
import functools

import jax
import jax.numpy as jnp
from jax import lax
from jax.experimental import pallas as pl
from jax.experimental.pallas import tpu as pltpu

F32 = jnp.float32
BF16 = jnp.bfloat16

D_MODEL = 1024
GRID_W = 64
POOL_WIDTH = 512
POOL_WINDOWS = (2, 4, 8, 16)
POOL_GC = POOL_WIDTH // len(POOL_WINDOWS)
N_HEADS = 16
N_KV_HEADS = 4
GROUP = N_HEADS // N_KV_HEADS
HEAD_DIM = 64
ATTN_WIDTH = N_HEADS * HEAD_DIM
KV_WIDTH = N_KV_HEADS * HEAD_DIM
Q_BLOCK = 128
ROPE_BASE = 10000.0
ROPE_HALF = 16
EPS = 1e-6
NEG_INF = -1e30
F32_SUBLANES = 8
BF16_SUBLANES = 16
POOL_HALO = BF16_SUBLANES
LOG2_E = 1.4426950408889634
PROJ_ROWS = 512
TAIL_ROWS = 512
ATTN_SCORE_SLOTS = 4
ATTN_BLOCKS_PER_STEP = 4

_XA, _GA, _Q, _K, _V, _GB, _GM = 0, 512, 1024, 2048, 2304, 2560, 3584
IN_WIDTH = 5632
_T_XA, _T_GA, _T_K, _T_GB, _T_GM, _T_END = 0, 512, 1024, 1280, 2304, 4352

VMEM_LIMIT_BYTES = 48 * 1024 * 1024


def _const_spec(shape):
    nd = len(shape)
    return pl.BlockSpec(shape, lambda *_: (0,) * nd, pipeline_mode=pl.Buffered(1))


def _sigmoid(v):
    return 0.5 * jnp.tanh(0.5 * v) + 0.5


def _silu(v):
    h = 0.5 * v
    return h * jnp.tanh(h) + h


def _adaln_kernel(c_ref, w_ref, b_ref, o_ref):
    o_ref[...] = jnp.dot(_silu(c_ref[...]), w_ref[...], preferred_element_type=F32) + b_ref[...]


def _adaln(c_all, w_mod, b_mod):
    rows = c_all.shape[0]
    n = w_mod.shape[1]
    bn = D_MODEL
    return pl.pallas_call(
        _adaln_kernel,
        grid=(n // bn,),
        in_specs=[pl.BlockSpec((rows, D_MODEL), lambda j: (0, 0)),
                  pl.BlockSpec((D_MODEL, bn), lambda j: (0, j)),
                  pl.BlockSpec((1, bn), lambda j: (0, j))],
        out_specs=pl.BlockSpec((rows, bn), lambda j: (0, j)),
        out_shape=jax.ShapeDtypeStruct((rows, n), F32),
        compiler_params=pltpu.CompilerParams(dimension_semantics=("arbitrary",),
                                             vmem_limit_bytes=VMEM_LIMIT_BYTES),
        name="adaln",
    )(c_all, w_mod, b_mod)


def _norm_modulate(x, g, shift, scale):
    ms = jnp.mean(x * x, axis=-1, keepdims=True)
    y = x * lax.rsqrt(ms + EPS) * g
    return y * (1.0 + scale) + shift


def _dot_t(w_t, h):
    return lax.dot_general(w_t, h, (((1,), (1,)), ((), ())), preferred_element_type=F32)


def _ctx_kv_kernel(ctx_ref, g_ref, shift_ref, scale_ref, wk_ref, wvt_ref, k_ref, vt_ref):
    h = _norm_modulate(ctx_ref[0], g_ref[...], shift_ref[...], scale_ref[...]).astype(BF16)
    k_ref[0] = jnp.dot(h, wk_ref[...], preferred_element_type=F32).astype(BF16)
    vt_ref[0] = _dot_t(wvt_ref[...], h).astype(BF16)


def _ctx_kv(ctx, g, shift, scale, wk, wvt):
    B, C, _ = ctx.shape
    return pl.pallas_call(
        _ctx_kv_kernel,
        grid=(B,),
        in_specs=[pl.BlockSpec((1, C, D_MODEL), lambda b: (b, 0, 0)),
                  _const_spec((1, D_MODEL)), _const_spec((1, D_MODEL)), _const_spec((1, D_MODEL)),
                  _const_spec((D_MODEL, KV_WIDTH)), _const_spec((KV_WIDTH, D_MODEL))],
        out_specs=[pl.BlockSpec((1, C, KV_WIDTH), lambda b: (b, 0, 0)),
                   pl.BlockSpec((1, KV_WIDTH, C), lambda b: (b, 0, 0))],
        out_shape=[jax.ShapeDtypeStruct((B, C, KV_WIDTH), BF16),
                   jax.ShapeDtypeStruct((B, KV_WIDTH, C), BF16)],
        compiler_params=pltpu.CompilerParams(dimension_semantics=("arbitrary",),
                                             vmem_limit_bytes=VMEM_LIMIT_BYTES),
        name="ctx_kv",
    )(ctx, g, shift, scale, wk, wvt)


def _proj_kernel(x_ref, shift_ref, scale_ref, g_ref, wtok_ref, wft_ref,
                 cosk_ref, sink_ref, cosq_ref, sinq_ref,
                 xa_ref, ga_ref, k_ref, gb_ref, gm_ref, qt_ref, vt_ref):
    tm = x_ref.shape[1]
    h = _norm_modulate(x_ref[0], g_ref[...], shift_ref[0], scale_ref[0]).astype(BF16)

    def tok(lo, hi):
        return jnp.dot(h, wtok_ref[:, lo:hi], preferred_element_type=F32)

    xa_ref[0] = tok(_T_XA, _T_GA).astype(BF16)
    ga_ref[0] = _silu(tok(_T_GA, _T_K)).astype(BF16)
    gb_ref[0] = _silu(tok(_T_GB, _T_GM)).astype(BF16)
    gm_ref[0] = _sigmoid(tok(_T_GM, _T_END)).astype(BF16)

    kf = tok(_T_K, _T_GB)
    lane = lax.broadcasted_iota(jnp.int32, kf.shape, 1)
    partner = jnp.where((lane & (2 * ROPE_HALF - 1)) < ROPE_HALF,
                        pltpu.roll(kf, KV_WIDTH - ROPE_HALF, 1),
                        pltpu.roll(kf, ROPE_HALF, 1))
    k_ref[0] = (kf * cosk_ref[...] + partner * sink_ref[...]).astype(BF16)

    qt = _dot_t(wft_ref[0:ATTN_WIDTH, :], h)
    q4 = qt.reshape(ATTN_WIDTH // (2 * ROPE_HALF), 2, ROPE_HALF, tm)
    qp = jnp.concatenate([q4[:, 1:2], q4[:, 0:1]], axis=1).reshape(ATTN_WIDTH, tm)
    cosq = jnp.tile(cosq_ref[...], (N_HEADS, 1))
    sinq = jnp.tile(sinq_ref[...], (N_HEADS, 1))
    qt_ref[0] = (qt * cosq + qp * sinq).astype(BF16)

    vt_ref[0] = _dot_t(wft_ref[ATTN_WIDTH:ATTN_WIDTH + KV_WIDTH, :], h).astype(BF16)


def _proj(x, shift, scale, g, wtok, wft, cosk, sink, cosq, sinq, tm):
    B, L, _ = x.shape
    nt = L // tm
    tok_map = lambda t, b: (b, t, 0)
    feat_map = lambda t, b: (b, 0, t)
    vec_map = lambda t, b: (b, 0, 0)
    return pl.pallas_call(
        _proj_kernel,
        grid=(nt, B),
        in_specs=[pl.BlockSpec((1, tm, D_MODEL), tok_map),
                  pl.BlockSpec((1, 1, D_MODEL), vec_map),
                  pl.BlockSpec((1, 1, D_MODEL), vec_map),
                  _const_spec((1, D_MODEL)),
                  _const_spec((D_MODEL, _T_END)),
                  _const_spec((ATTN_WIDTH + KV_WIDTH, D_MODEL)),
                  pl.BlockSpec((tm, KV_WIDTH), lambda t, b: (t, 0)),
                  pl.BlockSpec((tm, KV_WIDTH), lambda t, b: (t, 0)),
                  pl.BlockSpec((HEAD_DIM, tm), lambda t, b: (0, t)),
                  pl.BlockSpec((HEAD_DIM, tm), lambda t, b: (0, t))],
        out_specs=[pl.BlockSpec((1, tm, POOL_WIDTH), tok_map),
                   pl.BlockSpec((1, tm, POOL_WIDTH), tok_map),
                   pl.BlockSpec((1, tm, KV_WIDTH), tok_map),
                   pl.BlockSpec((1, tm, ATTN_WIDTH), tok_map),
                   pl.BlockSpec((1, tm, 2 * D_MODEL), tok_map),
                   pl.BlockSpec((1, ATTN_WIDTH, tm), feat_map),
                   pl.BlockSpec((1, KV_WIDTH, tm), feat_map)],
        out_shape=[jax.ShapeDtypeStruct((B, L, POOL_WIDTH), BF16),
                   jax.ShapeDtypeStruct((B, L, POOL_WIDTH), BF16),
                   jax.ShapeDtypeStruct((B, L, KV_WIDTH), BF16),
                   jax.ShapeDtypeStruct((B, L, ATTN_WIDTH), BF16),
                   jax.ShapeDtypeStruct((B, L, 2 * D_MODEL), BF16),
                   jax.ShapeDtypeStruct((B, ATTN_WIDTH, L), BF16),
                   jax.ShapeDtypeStruct((B, KV_WIDTH, L), BF16)],
        compiler_params=pltpu.CompilerParams(dimension_semantics=("arbitrary", "arbitrary"),
                                             vmem_limit_bytes=VMEM_LIMIT_BYTES),
        name="proj",
    )(x, shift, scale, g, wtok, wft, cosk, sink, cosq, sinq)


def _attn_kernel(qt_ref, kp_ref, kc_ref, kn_ref, vp_ref, vc_ref, vn_ref, kx_ref, vx_ref, sink_ref,
                 o_ref, ot_scr, s_scr):
    j = pl.program_id(1)
    nstep = pl.num_programs(1)
    nq = GROUP * Q_BLOCK
    qb = kc_ref.shape[1] // Q_BLOCK
    nctx = kx_ref.shape[1]

    kk = lax.broadcasted_iota(jnp.int32, (Q_BLOCK, nq), 0)
    qq = lax.broadcasted_iota(jnp.int32, (Q_BLOCK, nq), 1) & (Q_BLOCK - 1)
    tri_p = kk >= qq
    tri_n = kk <= qq
    ones_rows = jnp.ones((BF16_SUBLANES, 3 * Q_BLOCK + nctx), BF16)
    kx, vx = kx_ref[0], vx_ref[0]

    def blk(ref_c, ref_edge, u, lane_major):
        if u < 0 or u >= qb:
            return ref_edge[0]
        if lane_major:
            return ref_c[0, :, u * Q_BLOCK:(u + 1) * Q_BLOCK]
        return ref_c[0, u * Q_BLOCK:(u + 1) * Q_BLOCK, :]

    def scores(u, h, slot):
        q_h = jnp.concatenate(
            [qt_ref[0, (GROUP * h + g) * HEAD_DIM:(GROUP * h + g + 1) * HEAD_DIM,
                    u * Q_BLOCK:(u + 1) * Q_BLOCK] for g in range(GROUP)], axis=1)
        pieces = []
        if h > 0:
            pieces.append(jnp.zeros((h * HEAD_DIM, nq), BF16))
        pieces.append(q_h)
        if h < N_KV_HEADS - 1:
            pieces.append(jnp.zeros(((N_KV_HEADS - 1 - h) * HEAD_DIM, nq), BF16))
        q_pad = jnp.concatenate(pieces, axis=0)
        mask_p = tri_p & (j > 0) if u == 0 else tri_p
        mask_n = tri_n & (j < nstep - 1) if u == qb - 1 else tri_n
        kp = blk(kc_ref, kp_ref, u - 1, False)
        kc = blk(kc_ref, kp_ref, u, False)
        kn = blk(kc_ref, kn_ref, u + 1, False)
        s = jnp.dot(jnp.concatenate([kp, kc, kn, kx], axis=0), q_pad, preferred_element_type=F32)
        s_p = jnp.where(mask_p, s[0:Q_BLOCK], NEG_INF)
        s_c = s[Q_BLOCK:2 * Q_BLOCK]
        s_n = jnp.where(mask_n, s[2 * Q_BLOCK:3 * Q_BLOCK], NEG_INF)
        s_x = s[3 * Q_BLOCK:]
        s_all = (s_p, s_c, s_n, s_x)
        m8 = functools.reduce(jnp.maximum, [jnp.max(s.reshape(-1, F32_SUBLANES, nq), axis=0) for s in s_all])
        r0 = 0
        for s in s_all:
            s_scr[slot, r0:r0 + s.shape[0], :] = s
            r0 += s.shape[0]
        return m8

    def finish(u, h, slot, m8):
        sink = sink_ref[h]
        m = jnp.maximum(jnp.max(m8, axis=0, keepdims=True), sink)
        p_all = jnp.exp2(s_scr[slot] - m).astype(BF16)
        rows = slice(h * HEAD_DIM, (h + 1) * HEAD_DIM)
        v_aug = jnp.concatenate([blk(vc_ref, vp_ref, u - 1, True)[rows], blk(vc_ref, vp_ref, u, True)[rows],
                                 blk(vc_ref, vn_ref, u + 1, True)[rows], vx[rows]], axis=1)
        v_aug = jnp.concatenate([v_aug, ones_rows], axis=0)
        o_aug = jnp.dot(v_aug, p_all, preferred_element_type=F32)
        denom = o_aug[HEAD_DIM:HEAD_DIM + 1, :] + jnp.exp2(sink - m)
        o_t = o_aug[:HEAD_DIM, :] * (1.0 / denom)
        for g in range(GROUP):
            r = (GROUP * h + g) * HEAD_DIM
            ot_scr[r:r + HEAD_DIM, u * Q_BLOCK:(u + 1) * Q_BLOCK] = o_t[:, g * Q_BLOCK:(g + 1) * Q_BLOCK]

    units = [(u, h) for u in range(qb) for h in range(N_KV_HEADS)]
    nslot = s_scr.shape[0]
    ahead = nslot - 1
    pending = [scores(*un, n % nslot) for n, un in enumerate(units[:ahead])]
    for n, un in enumerate(units):
        if n + ahead < len(units):
            pending.append(scores(*units[n + ahead], (n + ahead) % nslot))
        finish(*un, n % nslot, pending.pop(0))
        if un[1] == N_KV_HEADS - 1:
            u = un[0]
            o_ref[0, u * Q_BLOCK:(u + 1) * Q_BLOCK, :] = (
                ot_scr[:, u * Q_BLOCK:(u + 1) * Q_BLOCK].T.astype(BF16))


def _attn(qt, k, vt, kx, vxt, sink_rows, qb):
    B, L, _ = k.shape
    C = kx.shape[1]
    nblk = L // Q_BLOCK
    nstep = nblk // qb
    tq = qb * Q_BLOCK
    prev = lambda j: jnp.maximum(j * qb - 1, 0)
    nxt = lambda j: jnp.minimum((j + 1) * qb, nblk - 1)
    return pl.pallas_call(
        _attn_kernel,
        grid=(B, nstep),
        in_specs=[pl.BlockSpec((1, ATTN_WIDTH, tq), lambda b, j: (b, 0, j)),
                  pl.BlockSpec((1, Q_BLOCK, KV_WIDTH), lambda b, j: (b, prev(j), 0)),
                  pl.BlockSpec((1, tq, KV_WIDTH), lambda b, j: (b, j, 0)),
                  pl.BlockSpec((1, Q_BLOCK, KV_WIDTH), lambda b, j: (b, nxt(j), 0)),
                  pl.BlockSpec((1, KV_WIDTH, Q_BLOCK), lambda b, j: (b, 0, prev(j))),
                  pl.BlockSpec((1, KV_WIDTH, tq), lambda b, j: (b, 0, j)),
                  pl.BlockSpec((1, KV_WIDTH, Q_BLOCK), lambda b, j: (b, 0, nxt(j))),
                  pl.BlockSpec((1, C, KV_WIDTH), lambda b, j: (b, 0, 0)),
                  pl.BlockSpec((1, KV_WIDTH, C), lambda b, j: (b, 0, 0)),
                  _const_spec((N_KV_HEADS, 1, GROUP * Q_BLOCK))],
        out_specs=pl.BlockSpec((1, tq, ATTN_WIDTH), lambda b, j: (b, j, 0)),
        out_shape=jax.ShapeDtypeStruct((B, L, ATTN_WIDTH), BF16),
        scratch_shapes=[pltpu.VMEM((ATTN_WIDTH, tq), F32),
                        pltpu.VMEM((ATTN_SCORE_SLOTS, 3 * Q_BLOCK + C, GROUP * Q_BLOCK), F32)],
        compiler_params=pltpu.CompilerParams(dimension_semantics=("arbitrary", "arbitrary"),
                                             vmem_limit_bytes=VMEM_LIMIT_BYTES),
        name="attn",
    )(qt, k, k, k, vt, vt, vt, kx, vxt, sink_rows)


def _tail_kernel(x_ref, gate_ref, xa_ref, xap_ref, xan_ref, ga_ref, o_ref, gb_ref, gm_ref,
                 pw_ref, ps_ref, wa_ref, wb_ref, wo_ref, gpost_ref, out_ref, ext_scr, *, seq_len):
    t = pl.program_id(1)
    nt = pl.num_programs(1)
    tm = x_ref.shape[1]

    ext_scr[0:POOL_HALO, :] = jnp.where(t > 0, xap_ref[0].astype(F32), 0.0)
    ext_scr[POOL_HALO:POOL_HALO + tm, :] = xa_ref[0].astype(F32)
    ext_scr[POOL_HALO + tm:, :] = jnp.where(t < nt - 1, xan_ref[0].astype(F32), 0.0)
    pos = t * tm + lax.broadcasted_iota(jnp.int32, (tm, 1), 0)
    ys = []
    for gi, w in enumerate(POOL_WINDOWS):
        cols = slice(gi * POOL_GC, (gi + 1) * POOL_GC)
        span, width = ext_scr[:, cols], 1
        while width < w // 2:
            n = span.shape[0] - width
            span = span[0:n] + span[width:width + n]
            width *= 2
        acc = span[POOL_HALO - w // 2:POOL_HALO - w // 2 + tm] + span[POOL_HALO:POOL_HALO + tm]
        cnt = (jnp.minimum(pos + w // 2, seq_len) - jnp.maximum(pos - w // 2, 0)).astype(F32)
        pooled = acc * (1.0 / cnt) - ext_scr[POOL_HALO:POOL_HALO + tm, cols]
        ys.append(jnp.dot(pooled.astype(BF16), pw_ref[gi], preferred_element_type=F32))
    y_a = jnp.concatenate(ys, axis=1) * ps_ref[...] * ga_ref[0].astype(F32)
    z_a = jnp.dot(y_a.astype(BF16), wa_ref[...], preferred_element_type=F32)

    z_b = jnp.dot(o_ref[0] * gb_ref[0], wb_ref[...], preferred_element_type=F32)

    gsig = gm_ref[0].astype(F32)
    merged = gsig[:, :D_MODEL] * z_a + gsig[:, D_MODEL:] * z_b
    out = jnp.dot(merged.astype(BF16), wo_ref[...], preferred_element_type=F32)

    ms = jnp.mean(out * out, axis=-1, keepdims=True)
    normed = out * lax.rsqrt(ms + EPS) * gpost_ref[...]
    out_ref[0] = x_ref[0] + gate_ref[0] * normed


def _tail(x, gate, xa, ga, o, gb, gm, pool_w, pool_scale, wa, wb, wo, gpost, tm):
    B, L, _ = x.shape
    nt = L // tm
    hb = tm // POOL_HALO
    nhalo = L // POOL_HALO
    tok_map = lambda b, t: (b, t, 0)
    return pl.pallas_call(
        functools.partial(_tail_kernel, seq_len=L),
        grid=(B, nt),
        in_specs=[pl.BlockSpec((1, tm, D_MODEL), tok_map),
                  pl.BlockSpec((1, 1, D_MODEL), lambda b, t: (b, 0, 0)),
                  pl.BlockSpec((1, tm, POOL_WIDTH), tok_map),
                  pl.BlockSpec((1, POOL_HALO, POOL_WIDTH),
                               lambda b, t: (b, jnp.maximum(t * hb - 1, 0), 0)),
                  pl.BlockSpec((1, POOL_HALO, POOL_WIDTH),
                               lambda b, t: (b, jnp.minimum((t + 1) * hb, nhalo - 1), 0)),
                  pl.BlockSpec((1, tm, POOL_WIDTH), tok_map),
                  pl.BlockSpec((1, tm, ATTN_WIDTH), tok_map),
                  pl.BlockSpec((1, tm, ATTN_WIDTH), tok_map),
                  pl.BlockSpec((1, tm, 2 * D_MODEL), tok_map),
                  _const_spec((len(POOL_WINDOWS), POOL_GC, POOL_GC)),
                  _const_spec((1, POOL_WIDTH)),
                  _const_spec((POOL_WIDTH, D_MODEL)),
                  _const_spec((ATTN_WIDTH, D_MODEL)),
                  _const_spec((D_MODEL, D_MODEL)),
                  _const_spec((1, D_MODEL))],
        out_specs=pl.BlockSpec((1, tm, D_MODEL), tok_map),
        out_shape=jax.ShapeDtypeStruct((B, L, D_MODEL), F32),
        scratch_shapes=[pltpu.VMEM((tm + 2 * POOL_HALO, POOL_WIDTH), F32)],
        compiler_params=pltpu.CompilerParams(dimension_semantics=("arbitrary", "arbitrary"),
                                             vmem_limit_bytes=VMEM_LIMIT_BYTES),
        name="tail",
    )(x, gate, xa, xa, xa, ga, o, gb, gm, pool_w, pool_scale, wa, wb, wo, gpost)


def _rope_tables(L):
    rows = L // GRID_W
    row = jnp.repeat(jnp.arange(rows), GRID_W).astype(F32)
    col = jnp.tile(jnp.arange(GRID_W), rows).astype(F32)
    freqs = ROPE_BASE ** (-jnp.arange(ROPE_HALF, dtype=F32) / ROPE_HALF)
    ang_r = row[:, None] * freqs[None, :]
    ang_c = col[:, None] * freqs[None, :]
    cos = jnp.concatenate([jnp.cos(ang_r), jnp.cos(ang_r), jnp.cos(ang_c), jnp.cos(ang_c)], axis=1)
    sin = jnp.concatenate([-jnp.sin(ang_r), jnp.sin(ang_r), -jnp.sin(ang_c), jnp.sin(ang_c)], axis=1)
    return cos, sin


def kernel(x, c, ctx, c_ctx, w_mod, b_mod, norm_pre_g, norm_post_g, w_in, pool_w, pool_scale, sink,
           w_branch_a, w_branch_b, w_out):
    B, L, D = x.shape
    depth = w_in.shape[0]
    assert D == D_MODEL and L % Q_BLOCK == 0 and L % GRID_W == 0 and w_in.shape[2] == IN_WIDTH
    assert depth == 1, "context-stream update between layers is not implemented"
    tm = min(PROJ_ROWS, L)
    tm_tail = min(TAIL_ROWS, L)

    cos, sin = _rope_tables(L)
    cosk = jnp.tile(cos, (1, N_KV_HEADS))
    sink_k = jnp.tile(sin, (1, N_KV_HEADS))
    qscale = HEAD_DIM ** -0.5 * LOG2_E
    cosq = (cos * qscale).T
    sinq = (sin * qscale).T

    for i in range(depth):
        pad = (-(B + 1)) % 8
        c_all = jnp.concatenate([c, c_ctx[None, :], jnp.zeros((pad, D), F32)], axis=0)
        mod = _adaln(c_all, w_mod[i], b_mod[i][None, :])
        shift, scale, gate = (mod[:B, j * D:(j + 1) * D][:, None, :] for j in range(3))
        c_shift, c_scale = (mod[B:B + 1, j * D:(j + 1) * D] for j in range(2))

        w = w_in[i]
        wtok = jnp.concatenate([w[:, _XA:_Q], w[:, _K:_V], w[:, _GB:]], axis=1).astype(BF16)
        wft = jnp.concatenate([w[:, _Q:_K], w[:, _V:_GB]], axis=1).T.astype(BF16)
        g_pre = norm_pre_g[i][None, :]

        kx, vxt = _ctx_kv(ctx, g_pre, c_shift, c_scale,
                          w[:, _K:_V].astype(BF16), w[:, _V:_GB].T.astype(BF16))
        xa, ga, k, gb, gm, qt, vt = _proj(x, shift, scale, g_pre, wtok, wft, cosk, sink_k, cosq, sinq, tm)

        sink_rows = jnp.repeat(sink[i].astype(F32) * LOG2_E, Q_BLOCK).reshape(N_KV_HEADS, 1, GROUP * Q_BLOCK)
        o = _attn(qt, k, vt, kx, vxt, sink_rows, ATTN_BLOCKS_PER_STEP)

        x = _tail(x, gate, xa, ga, o, gb, gm,
                  pool_w[i].astype(BF16), pool_scale[i][None, :],
                  w_branch_a[i].astype(BF16), w_branch_b[i].astype(BF16), w_out[i].astype(BF16),
                  norm_post_g[i][None, :], tm_tail)
    return x
```

```python
import functools

import jax
import jax.numpy as jnp
from jax import lax
from jax.experimental import pallas as pl
from jax.experimental.pallas import tpu as pltpu

F32 = jnp.float32
BF16 = jnp.bfloat16

D_MODEL = 1024
GRID_W = 64
POOL_WIDTH = 512
POOL_WINDOWS = (2, 4, 8, 16)
POOL_GC = POOL_WIDTH // len(POOL_WINDOWS)
N_HEADS = 16
N_KV_HEADS = 4
GROUP = N_HEADS // N_KV_HEADS
HEAD_DIM = 64
ATTN_WIDTH = N_HEADS * HEAD_DIM
KV_WIDTH = N_KV_HEADS * HEAD_DIM
Q_BLOCK = 128
ROPE_BASE = 10000.0
ROPE_HALF = 16
EPS = 1e-6
NEG_INF = -1e30
F32_SUBLANES = 8
BF16_SUBLANES = 16
POOL_HALO = BF16_SUBLANES
LOG2_E = 1.4426950408889634
PROJ_ROWS = 512
PROJ_CHUNK = 256
PROJ_CHUNKS_BEFORE_KV = 3
PROJ_CHUNKS_AFTER = 4
TAIL_SLAB = 256
CTX_BATCH_PER_STEP = 4
ATTN_SCORE_SLOTS = 5

_XA, _GA, _Q, _K, _V, _GB, _GM = 0, 512, 1024, 2048, 2304, 2560, 3584
IN_WIDTH = 5632
_T_XA, _T_GA, _T_K, _T_GB, _T_GM, _T_END = 0, 512, 1024, 1280, 2304, 4352

VMEM_LIMIT_BYTES = 48 * 1024 * 1024
LAYER_VMEM_LIMIT_BYTES = 60 * 1024 * 1024


def _const_spec(shape):
    nd = len(shape)
    return pl.BlockSpec(shape, lambda *_: (0,) * nd, pipeline_mode=pl.Buffered(1))


def _sigmoid(v):
    return 0.5 * jnp.tanh(0.5 * v) + 0.5


def _silu(v):
    h = 0.5 * v
    return h * jnp.tanh(h) + h


def _adaln_kernel(c_ref, w_ref, b_ref, o_ref):
    o_ref[...] = jnp.dot(_silu(c_ref[...]), w_ref[...], preferred_element_type=F32) + b_ref[...]


def _adaln(c_all, w_mod, b_mod):
    rows = c_all.shape[0]
    n = w_mod.shape[1]
    bn = D_MODEL
    return pl.pallas_call(
        _adaln_kernel,
        grid=(n // bn,),
        in_specs=[pl.BlockSpec((rows, D_MODEL), lambda j: (0, 0)),
                  pl.BlockSpec((D_MODEL, bn), lambda j: (0, j)),
                  pl.BlockSpec((1, bn), lambda j: (0, j))],
        out_specs=pl.BlockSpec((rows, bn), lambda j: (0, j)),
        out_shape=jax.ShapeDtypeStruct((rows, n), F32),
        compiler_params=pltpu.CompilerParams(dimension_semantics=("arbitrary",),
                                             vmem_limit_bytes=VMEM_LIMIT_BYTES),
        name="adaln",
    )(c_all, w_mod, b_mod)


def _norm_modulate(x, g, shift, scale):
    ms = jnp.mean(x * x, axis=-1, keepdims=True)
    y = x * lax.rsqrt(ms + EPS) * g
    return y * (1.0 + scale) + shift


def _dot_t(w_t, h):
    return lax.dot_general(w_t, h, (((1,), (1,)), ((), ())), preferred_element_type=F32)


def _ctx_kv_kernel(ctx_ref, g_ref, shift_ref, scale_ref, wk_ref, wvt_ref, k_ref, vt_ref):
    nb, C, _ = ctx_ref.shape
    h = _norm_modulate(ctx_ref[...].reshape(nb * C, D_MODEL), g_ref[...], shift_ref[...], scale_ref[...])
    h = h.astype(BF16)
    k_ref[...] = jnp.dot(h, wk_ref[...], preferred_element_type=F32).astype(BF16).reshape(nb, C, KV_WIDTH)
    for b in range(nb):
        vt_ref[b] = _dot_t(wvt_ref[...], h[b * C:(b + 1) * C]).astype(BF16)


def _ctx_kv(ctx, g, shift, scale, wk, wvt):
    B, C, _ = ctx.shape
    nb = CTX_BATCH_PER_STEP if B % CTX_BATCH_PER_STEP == 0 else 1
    return pl.pallas_call(
        _ctx_kv_kernel,
        grid=(B // nb,),
        in_specs=[pl.BlockSpec((nb, C, D_MODEL), lambda b: (b, 0, 0)),
                  _const_spec((1, D_MODEL)), _const_spec((1, D_MODEL)), _const_spec((1, D_MODEL)),
                  _const_spec((D_MODEL, KV_WIDTH)), _const_spec((KV_WIDTH, D_MODEL))],
        out_specs=[pl.BlockSpec((nb, C, KV_WIDTH), lambda b: (b, 0, 0)),
                   pl.BlockSpec((nb, KV_WIDTH, C), lambda b: (b, 0, 0))],
        out_shape=[jax.ShapeDtypeStruct((B, C, KV_WIDTH), BF16),
                   jax.ShapeDtypeStruct((B, KV_WIDTH, C), BF16)],
        compiler_params=pltpu.CompilerParams(dimension_semantics=("arbitrary",),
                                             vmem_limit_bytes=VMEM_LIMIT_BYTES),
        name="ctx_kv",
    )(ctx, g, shift, scale, wk, wvt)


def _layer_kernel(x_ref, shift_ref, scale_ref, x0_ref, shift0_ref, scale0_ref, g_ref, wtok_ref, wft_ref,
                  cosk_ref, sink_ref, cosq_ref, sinq_ref, kx_ref, vx_ref, kxe_ref, srow_ref,
                  xres_ref, gate_ref, pw_ref, ps_ref, wa_ref, wb_ref, wo_ref, gpost_ref,
                  out_ref,
                  h_scr, h_new, qt_ring, k_ring, vt_ring, k_new, vt_new, ot_scr, s_scr, m_scr,
                  xa_ring, ga_ring, gb_ring, gm_ring, o_scr, halo_scr, ext_scr, *, tiles_per_seq, seq_len):
    s = pl.program_id(0)
    tm = x_ref.shape[1]
    qb = tm // Q_BLOCK
    nq = GROUP * Q_BLOCK
    nctx = kx_ref.shape[1]

    @pl.when(s == 0)
    def _():
        for ring in (qt_ring, k_ring, vt_ring, xa_ring, ga_ring, gb_ring, gm_ring, s_scr, m_scr):
            ring[...] = jnp.zeros_like(ring)

    wq = s % 2
    rq = 1 - wq
    halo_scr[...] = xa_ring[wq, tm - POOL_HALO:tm, :]
    ck = (s + 1) % 2
    pk = s % 2
    t_a = jnp.maximum(s - 1, 0) % tiles_per_seq
    has_prev = t_a > 0
    has_next = t_a < tiles_per_seq - 1
    has_prev_early = s % tiles_per_seq > 0

    @pl.when(s == 0)
    def _():
        h_scr[...] = _norm_modulate(x0_ref[0], g_ref[...], shift0_ref[0], scale0_ref[0]).astype(BF16)

    h_new[...] = _norm_modulate(x_ref[0], g_ref[...], shift_ref[0], scale_ref[0]).astype(BF16)

    def tok(lo, hi):
        return jnp.dot(h_scr[...], wtok_ref[:, lo:hi], preferred_element_type=F32)

    def proj_k():
        kf = tok(_T_K, _T_GB)
        lane = lax.broadcasted_iota(jnp.int32, kf.shape, 1)
        partner = jnp.where((lane & (2 * ROPE_HALF - 1)) < ROPE_HALF,
                            pltpu.roll(kf, KV_WIDTH - ROPE_HALF, 1),
                            pltpu.roll(kf, ROPE_HALF, 1))
        k_new[...] = (kf * cosk_ref[...] + partner * sink_ref[...]).astype(BF16)

    def proj_v():
        vt_new[...] = _dot_t(wft_ref[ATTN_WIDTH:ATTN_WIDTH + KV_WIDTH, :], h_scr[...]).astype(BF16)

    def proj_q(r0):
        qt = _dot_t(wft_ref[r0:r0 + PROJ_CHUNK, :], h_scr[...])
        q4 = qt.reshape(PROJ_CHUNK // (2 * ROPE_HALF), 2, ROPE_HALF, tm)
        qp = jnp.concatenate([q4[:, 1:2], q4[:, 0:1]], axis=1).reshape(PROJ_CHUNK, tm)
        cosq = jnp.tile(cosq_ref[...], (PROJ_CHUNK // HEAD_DIM, 1))
        sinq = jnp.tile(sinq_ref[...], (PROJ_CHUNK // HEAD_DIM, 1))
        qt_ring[wq, r0:r0 + PROJ_CHUNK, :] = (qt * cosq + qp * sinq).astype(BF16)

    def proj_tok(ring, base, lo, act):
        v = tok(base + lo, base + lo + PROJ_CHUNK)
        ring[wq, :, lo:lo + PROJ_CHUNK] = (v if act is None else act(v)).astype(BF16)

    def tok_chunks(ring, base, width, act):
        return [functools.partial(proj_tok, ring, base, lo, act) for lo in range(0, width, PROJ_CHUNK)]

    gm_chunks = tok_chunks(gm_ring, _T_GM, 2 * D_MODEL, _sigmoid)
    chunks = gm_chunks[:PROJ_CHUNKS_BEFORE_KV] + [proj_k, proj_v] + gm_chunks[PROJ_CHUNKS_BEFORE_KV:]
    kv_done = PROJ_CHUNKS_BEFORE_KV + 2
    chunks += tok_chunks(xa_ring, _T_XA, POOL_WIDTH, None)
    xa_done = len(chunks)
    chunks += tok_chunks(gb_ring, _T_GB, ATTN_WIDTH, _silu)
    chunks += tok_chunks(ga_ring, _T_GA, POOL_WIDTH, _silu)
    chunks += [functools.partial(proj_q, r0) for r0 in range(0, ATTN_WIDTH, PROJ_CHUNK)]

    kk = lax.broadcasted_iota(jnp.int32, (Q_BLOCK, nq), 0)
    qq = lax.broadcasted_iota(jnp.int32, (Q_BLOCK, nq), 1) & (Q_BLOCK - 1)
    tri_p = kk >= qq
    tri_n = kk <= qq
    ones_rows = jnp.ones((BF16_SUBLANES, 3 * Q_BLOCK + nctx), BF16)
    kx, vx = kx_ref[0], vx_ref[0]

    def k_blk(u, early):
        if u < 0:
            return k_ring[ck if early else pk, (qb - 1) * Q_BLOCK:qb * Q_BLOCK, :]
        if early:
            return k_new[u * Q_BLOCK:(u + 1) * Q_BLOCK, :]
        if u >= qb:
            return k_new[0:Q_BLOCK, :]
        return k_ring[ck, u * Q_BLOCK:(u + 1) * Q_BLOCK, :]

    def v_blk(u, rows):
        if u < 0:
            return vt_ring[pk, rows, (qb - 1) * Q_BLOCK:qb * Q_BLOCK]
        if u >= qb:
            return vt_new[rows, 0:Q_BLOCK]
        return vt_ring[ck, rows, u * Q_BLOCK:(u + 1) * Q_BLOCK]

    def scores(u, h, slot, early=False):
        assert not early or u + 1 < qb
        q_h = jnp.concatenate(
            [qt_ring[wq if early else rq, (GROUP * h + g) * HEAD_DIM:(GROUP * h + g + 1) * HEAD_DIM,
                     u * Q_BLOCK:(u + 1) * Q_BLOCK] for g in range(GROUP)], axis=1)
        pieces = []
        if h > 0:
            pieces.append(jnp.zeros((h * HEAD_DIM, nq), BF16))
        pieces.append(q_h)
        if h < N_KV_HEADS - 1:
            pieces.append(jnp.zeros(((N_KV_HEADS - 1 - h) * HEAD_DIM, nq), BF16))
        q_pad = jnp.concatenate(pieces, axis=0)
        mask_p = tri_p & (has_prev_early if early else has_prev) if u == 0 else tri_p
        mask_n = tri_n & has_next if u == qb - 1 else tri_n
        keys = [k_blk(u - 1, early), k_blk(u, early), k_blk(u + 1, early), kxe_ref[0] if early else kx]
        sc = jnp.dot(jnp.concatenate(keys, axis=0), q_pad, preferred_element_type=F32)
        s_all = (jnp.where(mask_p, sc[0:Q_BLOCK], NEG_INF), sc[Q_BLOCK:2 * Q_BLOCK],
                 jnp.where(mask_n, sc[2 * Q_BLOCK:3 * Q_BLOCK], NEG_INF), sc[3 * Q_BLOCK:])
        m8 = functools.reduce(jnp.maximum, [jnp.max(t.reshape(-1, F32_SUBLANES, nq), axis=0) for t in s_all])
        r0 = 0
        for t in s_all:
            s_scr[slot, r0:r0 + t.shape[0], :] = t
            r0 += t.shape[0]
        m_scr[slot] = m8

    def finish(u, h, slot):
        sink = srow_ref[h]
        m = jnp.maximum(jnp.max(m_scr[slot], axis=0, keepdims=True), sink)
        p_all = jnp.exp2(s_scr[slot] - m).astype(BF16)
        rows = slice(h * HEAD_DIM, (h + 1) * HEAD_DIM)
        v_aug = jnp.concatenate([v_blk(u - 1, rows), v_blk(u, rows), v_blk(u + 1, rows), vx[rows]], axis=1)
        v_aug = jnp.concatenate([v_aug, ones_rows], axis=0)
        o_aug = jnp.dot(v_aug, p_all, preferred_element_type=F32)
        denom = o_aug[HEAD_DIM:HEAD_DIM + 1, :] + jnp.exp2(sink - m)
        o_t = o_aug[:HEAD_DIM, :] * (1.0 / denom)
        for g in range(GROUP):
            r = (GROUP * h + g) * HEAD_DIM
            ot_scr[r:r + HEAD_DIM, u * Q_BLOCK:(u + 1) * Q_BLOCK] = o_t[:, g * Q_BLOCK:(g + 1) * Q_BLOCK]

    slab = min(TAIL_SLAB, tm)
    post_scale = gate_ref[0] * gpost_ref[...]

    def fill_ext():
        ext_scr[0:POOL_HALO, :] = jnp.where(has_prev, halo_scr[...].astype(F32), 0.0)
        ext_scr[POOL_HALO:POOL_HALO + tm, :] = xa_ring[rq].astype(F32)
        ext_scr[POOL_HALO + tm:, :] = jnp.where(has_next, xa_ring[wq, 0:POOL_HALO, :].astype(F32), 0.0)

    def branch_b(r0):
        rows = slice(r0, r0 + slab)
        return jnp.dot(o_scr[rows, :] * gb_ring[rq, rows, :], wb_ref[...], preferred_element_type=F32)

    def branch_a(r0):
        pos = t_a * tm + r0 + lax.broadcasted_iota(jnp.int32, (slab, 1), 0)
        pooled = []
        for gi, w in enumerate(POOL_WINDOWS):
            cols = slice(gi * POOL_GC, (gi + 1) * POOL_GC)
            span, width = ext_scr[r0:r0 + slab + 2 * POOL_HALO, cols], 1
            while width < w // 2:
                n = span.shape[0] - width
                span = span[0:n] + span[width:width + n]
                width *= 2
            acc = span[POOL_HALO - w // 2:POOL_HALO - w // 2 + slab] + span[POOL_HALO:POOL_HALO + slab]
            cnt = (jnp.minimum(pos + w // 2, seq_len) - jnp.maximum(pos - w // 2, 0)).astype(F32)
            pooled.append((acc * (1.0 / cnt)
                           - ext_scr[POOL_HALO + r0:POOL_HALO + r0 + slab, cols]).astype(BF16))
        ys = [jnp.dot(jnp.concatenate(pooled[2 * j:2 * j + 2], axis=1), pw_ref[j],
                      preferred_element_type=F32) for j in range(len(POOL_WINDOWS) // 2)]
        y_a = jnp.concatenate(ys, axis=1) * ps_ref[...] * ga_ring[rq, r0:r0 + slab, :].astype(F32)
        return jnp.dot(y_a.astype(BF16), wa_ref[...], preferred_element_type=F32)

    def merge_out(r0, z_a, z_b):
        rows = slice(r0, r0 + slab)
        gsig = gm_ring[rq, rows, :].astype(F32)
        merged = gsig[:, :D_MODEL] * z_a + gsig[:, D_MODEL:] * z_b
        out = jnp.dot(merged.astype(BF16), wo_ref[...], preferred_element_type=F32)
        ms = jnp.mean(out * out, axis=-1, keepdims=True)
        out_ref[0, rows, :] = xres_ref[0, rows, :] + out * lax.rsqrt(ms + EPS) * post_scale

    units = [(u, h) for u in range(qb) for h in range(N_KV_HEADS)]
    nslot = s_scr.shape[0]
    ahead = nslot - 1
    spread = len(chunks) - PROJ_CHUNKS_AFTER
    issued_by = [-(-spread * (n + 1) // len(units)) for n in range(len(units))]
    assert tiles_per_seq == 1 or issued_by[(qb - 1) * N_KV_HEADS - ahead - 1] >= kv_done
    assert spread >= xa_done
    issued = 0
    for n, un in enumerate(units):
        if n + ahead < len(units):
            scores(*units[n + ahead], (n + ahead) % nslot)
        while issued < issued_by[n]:
            chunks.pop(0)()
            issued += 1
        finish(*un, n % nslot)
        if un[1] == N_KV_HEADS - 1:
            u = un[0]
            o_scr[u * Q_BLOCK:(u + 1) * Q_BLOCK, :] = ot_scr[:, u * Q_BLOCK:(u + 1) * Q_BLOCK].T.astype(BF16)

    def next_chunk():
        if chunks:
            chunks.pop(0)()

    fill_ext()
    starts = list(range(0, tm, slab))
    z_b = branch_b(starts[0])
    next_chunk()
    z_a = branch_a(starts[0])
    for i, r0 in enumerate(starts):
        next_chunk()
        if i + 1 < len(starts):
            z_b_next = branch_b(starts[i + 1])
            next_chunk()
            z_a_next = branch_a(starts[i + 1])
            next_chunk()
        merge_out(r0, z_a, z_b)
        if i + 1 < len(starts):
            z_a, z_b = z_a_next, z_b_next
    for chunk in chunks:
        chunk()
    h_scr[...] = h_new[...]
    for n, un in enumerate(units[:ahead]):
        scores(*un, n % nslot, early=True)
    k_ring[pk] = k_new[...]
    vt_ring[pk] = vt_new[...]


def _layer(x, shift, scale, gate, g, wtok, wft, cosk, sink, cosq, sinq, kx, vxt, sink_rows,
           pw_pairs, pool_scale, wa, wb, wo, gpost, tm):
    B, L, _ = x.shape
    C = kx.shape[1]
    tps = L // tm
    n_tiles = B * tps

    def p_tile(s):
        tile = jnp.minimum(s, n_tiles - 1)
        return tile // tps, tile % tps

    def a_tile(s):
        tile = jnp.maximum(s - 1, 0)
        return tile // tps, tile % tps

    tok_p = lambda s: (*p_tile(s), 0)
    tok_a = lambda s: (*a_tile(s), 0)
    vec_p = lambda s: (p_tile(s)[0], 0, 0)
    vec_a = lambda s: (a_tile(s)[0], 0, 0)
    tok_n = lambda s: (*p_tile(s + 1), 0)
    vec_n = lambda s: (p_tile(s + 1)[0], 0, 0)
    return pl.pallas_call(
        functools.partial(_layer_kernel, tiles_per_seq=tps, seq_len=L),
        grid=(n_tiles + 1,),
        in_specs=[pl.BlockSpec((1, tm, D_MODEL), tok_n),
                  pl.BlockSpec((1, 1, D_MODEL), vec_n),
                  pl.BlockSpec((1, 1, D_MODEL), vec_n),
                  _const_spec((1, tm, D_MODEL)),
                  _const_spec((1, 1, D_MODEL)),
                  _const_spec((1, 1, D_MODEL)),
                  _const_spec((1, D_MODEL)),
                  _const_spec((D_MODEL, _T_END)),
                  _const_spec((ATTN_WIDTH + KV_WIDTH, D_MODEL)),
                  pl.BlockSpec((tm, KV_WIDTH), lambda s: (p_tile(s)[1], 0)),
                  pl.BlockSpec((tm, KV_WIDTH), lambda s: (p_tile(s)[1], 0)),
                  pl.BlockSpec((HEAD_DIM, tm), lambda s: (0, p_tile(s)[1])),
                  pl.BlockSpec((HEAD_DIM, tm), lambda s: (0, p_tile(s)[1])),
                  pl.BlockSpec((1, C, KV_WIDTH), vec_a),
                  pl.BlockSpec((1, KV_WIDTH, C), vec_a),
                  pl.BlockSpec((1, C, KV_WIDTH), vec_p),
                  _const_spec((N_KV_HEADS, 1, GROUP * Q_BLOCK)),
                  pl.BlockSpec((1, tm, D_MODEL), tok_a),
                  pl.BlockSpec((1, 1, D_MODEL), vec_a),
                  _const_spec((len(POOL_WINDOWS) // 2, 2 * POOL_GC, 2 * POOL_GC)),
                  _const_spec((1, POOL_WIDTH)),
                  _const_spec((POOL_WIDTH, D_MODEL)),
                  _const_spec((ATTN_WIDTH, D_MODEL)),
                  _const_spec((D_MODEL, D_MODEL)),
                  _const_spec((1, D_MODEL))],
        out_specs=pl.BlockSpec((1, tm, D_MODEL), tok_a),
        out_shape=jax.ShapeDtypeStruct((B, L, D_MODEL), F32),
        scratch_shapes=[pltpu.VMEM((tm, D_MODEL), BF16),
                        pltpu.VMEM((tm, D_MODEL), BF16),
                        pltpu.VMEM((2, ATTN_WIDTH, tm), BF16),
                        pltpu.VMEM((2, tm, KV_WIDTH), BF16),
                        pltpu.VMEM((2, KV_WIDTH, tm), BF16),
                        pltpu.VMEM((tm, KV_WIDTH), BF16),
                        pltpu.VMEM((KV_WIDTH, tm), BF16),
                        pltpu.VMEM((ATTN_WIDTH, tm), F32),
                        pltpu.VMEM((ATTN_SCORE_SLOTS, 3 * Q_BLOCK + C, GROUP * Q_BLOCK), F32),
                        pltpu.VMEM((ATTN_SCORE_SLOTS, F32_SUBLANES, GROUP * Q_BLOCK), F32),
                        pltpu.VMEM((2, tm, POOL_WIDTH), BF16),
                        pltpu.VMEM((2, tm, POOL_WIDTH), BF16),
                        pltpu.VMEM((2, tm, ATTN_WIDTH), BF16),
                        pltpu.VMEM((2, tm, 2 * D_MODEL), BF16),
                        pltpu.VMEM((tm, ATTN_WIDTH), BF16),
                        pltpu.VMEM((POOL_HALO, POOL_WIDTH), BF16),
                        pltpu.VMEM((tm + 2 * POOL_HALO, POOL_WIDTH), F32)],
        compiler_params=pltpu.CompilerParams(dimension_semantics=("arbitrary",),
                                             vmem_limit_bytes=LAYER_VMEM_LIMIT_BYTES),
        name="layer",
    )(x, shift, scale, x, shift, scale, g, wtok, wft, cosk, sink, cosq, sinq, kx, vxt, kx, sink_rows,
      x, gate, pw_pairs, pool_scale, wa, wb, wo, gpost)


def _rope_tables(L):
    rows = L // GRID_W
    row = jnp.repeat(jnp.arange(rows), GRID_W).astype(F32)
    col = jnp.tile(jnp.arange(GRID_W), rows).astype(F32)
    freqs = ROPE_BASE ** (-jnp.arange(ROPE_HALF, dtype=F32) / ROPE_HALF)
    ang_r = row[:, None] * freqs[None, :]
    ang_c = col[:, None] * freqs[None, :]
    cos = jnp.concatenate([jnp.cos(ang_r), jnp.cos(ang_r), jnp.cos(ang_c), jnp.cos(ang_c)], axis=1)
    sin = jnp.concatenate([-jnp.sin(ang_r), jnp.sin(ang_r), -jnp.sin(ang_c), jnp.sin(ang_c)], axis=1)
    return cos, sin


def kernel(x, c, ctx, c_ctx, w_mod, b_mod, norm_pre_g, norm_post_g, w_in, pool_w, pool_scale, sink,
           w_branch_a, w_branch_b, w_out):
    B, L, D = x.shape
    depth = w_in.shape[0]
    assert D == D_MODEL and L % Q_BLOCK == 0 and L % GRID_W == 0 and w_in.shape[2] == IN_WIDTH
    assert depth == 1, "context-stream update between layers is not implemented"
    tm = min(PROJ_ROWS, L)

    cos, sin = _rope_tables(L)
    cosk = jnp.tile(cos, (1, N_KV_HEADS))
    sink_k = jnp.tile(sin, (1, N_KV_HEADS))
    qscale = HEAD_DIM ** -0.5 * LOG2_E
    cosq = (cos * qscale).T
    sinq = (sin * qscale).T

    for i in range(depth):
        pad = (-(B + 1)) % 8
        c_all = jnp.concatenate([c, c_ctx[None, :], jnp.zeros((pad, D), F32)], axis=0)
        mod = _adaln(c_all, w_mod[i], b_mod[i][None, :])
        shift, scale, gate = (mod[:B, j * D:(j + 1) * D][:, None, :] for j in range(3))
        c_shift, c_scale = (mod[B:B + 1, j * D:(j + 1) * D] for j in range(2))

        w = w_in[i].astype(BF16)
        wtok = jnp.concatenate([w[:, _XA:_Q], w[:, _K:_V], w[:, _GB:]], axis=1)
        wft = jnp.concatenate([w[:, _Q:_K], w[:, _V:_GB]], axis=1).T
        g_pre = norm_pre_g[i][None, :]

        kx, vxt = _ctx_kv(ctx, g_pre, c_shift, c_scale, w[:, _K:_V], wft[ATTN_WIDTH:])
        sink_rows = jnp.repeat(sink[i].astype(F32) * LOG2_E, Q_BLOCK).reshape(N_KV_HEADS, 1, GROUP * Q_BLOCK)
        pw = pool_w[i].astype(BF16)
        zero = jnp.zeros((POOL_GC, POOL_GC), BF16)
        pw_pairs = jnp.stack([jnp.block([[pw[2 * j], zero], [zero, pw[2 * j + 1]]])
                              for j in range(len(POOL_WINDOWS) // 2)])
        x = _layer(x, shift, scale, gate, g_pre, wtok, wft, cosk, sink_k, cosq, sinq, kx, vxt, sink_rows,
                   pw_pairs, pool_scale[i][None, :],
                   w_branch_a[i].astype(BF16), w_branch_b[i].astype(BF16), w_out[i].astype(BF16),
                   norm_post_g[i][None, :], tm)
    return x
```

```python
import functools

import jax
import jax.numpy as jnp
from jax import lax
from jax.experimental import pallas as pl
from jax.experimental.pallas import tpu as pltpu

F32 = jnp.float32
BF16 = jnp.bfloat16

D_MODEL = 1024
GRID_W = 64
POOL_WIDTH = 512
POOL_WINDOWS = (2, 4, 8, 16)
POOL_GC = POOL_WIDTH // len(POOL_WINDOWS)
N_HEADS = 16
N_KV_HEADS = 4
GROUP = N_HEADS // N_KV_HEADS
HEAD_DIM = 64
ATTN_WIDTH = N_HEADS * HEAD_DIM
KV_WIDTH = N_KV_HEADS * HEAD_DIM
Q_BLOCK = 128
ROPE_BASE = 10000.0
ROPE_HALF = 16
EPS = 1e-6
NEG_INF = -1e30
F32_SUBLANES = 8
BF16_SUBLANES = 16
POOL_HALO = BF16_SUBLANES
LOG2_E = 1.4426950408889634
PROJ_ROWS = 512
PROJ_CHUNK = 256
PROJ_CHUNKS_BEFORE_KV = 3
PROJ_CHUNKS_AFTER = 4
TAIL_SLAB = 512
WEIGHT_SLAB_ROWS = 256
CTX_BATCH_PER_STEP = 4
ATTN_SCORE_SLOTS = 5

_XA, _GA, _Q, _K, _V, _GB, _GM = 0, 512, 1024, 2048, 2304, 2560, 3584
IN_WIDTH = 5632
_T_XA, _T_GA, _T_K, _T_GB, _T_GM, _T_END = 0, 512, 1024, 1280, 2304, 4352

VMEM_LIMIT_BYTES = 48 * 1024 * 1024
LAYER_VMEM_LIMIT_BYTES = 60 * 1024 * 1024


def _const_spec(shape):
    nd = len(shape)
    return pl.BlockSpec(shape, lambda *_: (0,) * nd, pipeline_mode=pl.Buffered(1))


def _sigmoid(v):
    return 0.5 * jnp.tanh(0.5 * v) + 0.5


def _silu(v):
    h = 0.5 * v
    return h * jnp.tanh(h) + h


def _adaln_kernel(c_ref, w_ref, b_ref, o_ref):
    o_ref[...] = jnp.dot(_silu(c_ref[...]), w_ref[...], preferred_element_type=F32) + b_ref[...]


def _adaln(c_all, w_mod, b_mod):
    rows = c_all.shape[0]
    n = w_mod.shape[1]
    bn = D_MODEL
    return pl.pallas_call(
        _adaln_kernel,
        grid=(n // bn,),
        in_specs=[pl.BlockSpec((rows, D_MODEL), lambda j: (0, 0)),
                  pl.BlockSpec((D_MODEL, bn), lambda j: (0, j)),
                  pl.BlockSpec((1, bn), lambda j: (0, j))],
        out_specs=pl.BlockSpec((rows, bn), lambda j: (0, j)),
        out_shape=jax.ShapeDtypeStruct((rows, n), F32),
        compiler_params=pltpu.CompilerParams(dimension_semantics=("arbitrary",),
                                             vmem_limit_bytes=VMEM_LIMIT_BYTES),
        name="adaln",
    )(c_all, w_mod, b_mod)


def _weight_slabs_kernel(w_ref, wtok_ref, wft_ref):
    w = w_ref[...]
    wtok_ref[:, _T_XA:_T_K] = w[:, _XA:_Q].astype(BF16)
    wtok_ref[:, _T_K:_T_GB] = w[:, _K:_V].astype(BF16)
    wtok_ref[:, _T_GB:_T_END] = w[:, _GB:].astype(BF16)
    wft_ref[0:ATTN_WIDTH, :] = w[:, _Q:_K].T.astype(BF16)
    wft_ref[ATTN_WIDTH:ATTN_WIDTH + KV_WIDTH, :] = w[:, _V:_GB].T.astype(BF16)


def _weight_slabs(w):
    rows = WEIGHT_SLAB_ROWS
    return pl.pallas_call(
        _weight_slabs_kernel,
        grid=(D_MODEL // rows,),
        in_specs=[pl.BlockSpec((rows, IN_WIDTH), lambda r: (r, 0))],
        out_specs=[pl.BlockSpec((rows, _T_END), lambda r: (r, 0)),
                   pl.BlockSpec((ATTN_WIDTH + KV_WIDTH, rows), lambda r: (0, r))],
        out_shape=[jax.ShapeDtypeStruct((D_MODEL, _T_END), BF16),
                   jax.ShapeDtypeStruct((ATTN_WIDTH + KV_WIDTH, D_MODEL), BF16)],
        compiler_params=pltpu.CompilerParams(dimension_semantics=("arbitrary",),
                                             vmem_limit_bytes=VMEM_LIMIT_BYTES),
        name="weight_slabs",
    )(w)


def _norm_modulate(x, g, shift, scale):
    ms = jnp.mean(x * x, axis=-1, keepdims=True)
    y = x * lax.rsqrt(ms + EPS) * g
    return y * (1.0 + scale) + shift


def _dot_t(w_t, h):
    return lax.dot_general(w_t, h, (((1,), (1,)), ((), ())), preferred_element_type=F32)


def _ctx_kv_kernel(ctx_ref, g_ref, shift_ref, scale_ref, wk_ref, wvt_ref, k_ref, vt_ref):
    nb, C, _ = ctx_ref.shape
    h = _norm_modulate(ctx_ref[...].reshape(nb * C, D_MODEL), g_ref[...], shift_ref[...], scale_ref[...])
    h = h.astype(BF16)
    k_ref[...] = jnp.dot(h, wk_ref[...], preferred_element_type=F32).astype(BF16).reshape(nb, C, KV_WIDTH)
    for b in range(nb):
        vt_ref[b] = _dot_t(wvt_ref[...], h[b * C:(b + 1) * C]).astype(BF16)


def _ctx_kv(ctx, g, shift, scale, wk, wvt):
    B, C, _ = ctx.shape
    nb = CTX_BATCH_PER_STEP if B % CTX_BATCH_PER_STEP == 0 else 1
    return pl.pallas_call(
        _ctx_kv_kernel,
        grid=(B // nb,),
        in_specs=[pl.BlockSpec((nb, C, D_MODEL), lambda b: (b, 0, 0)),
                  _const_spec((1, D_MODEL)), _const_spec((1, D_MODEL)), _const_spec((1, D_MODEL)),
                  _const_spec((D_MODEL, KV_WIDTH)), _const_spec((KV_WIDTH, D_MODEL))],
        out_specs=[pl.BlockSpec((nb, C, KV_WIDTH), lambda b: (b, 0, 0)),
                   pl.BlockSpec((nb, KV_WIDTH, C), lambda b: (b, 0, 0))],
        out_shape=[jax.ShapeDtypeStruct((B, C, KV_WIDTH), BF16),
                   jax.ShapeDtypeStruct((B, KV_WIDTH, C), BF16)],
        compiler_params=pltpu.CompilerParams(dimension_semantics=("arbitrary",),
                                             vmem_limit_bytes=VMEM_LIMIT_BYTES),
        name="ctx_kv",
    )(ctx, g, shift, scale, wk, wvt)


def _layer_kernel(x_ref, shift_ref, scale_ref, x0_ref, shift0_ref, scale0_ref, g_ref, wtok_ref, wft_ref,
                  cosk_ref, sink_ref, cosq_ref, sinq_ref, kx_ref, vx_ref, kxe_ref, srow_ref,
                  xres_ref, gate_ref, pw_ref, ps_ref, wa_ref, wb_ref, wo_ref, gpost_ref,
                  out_ref,
                  h_scr, h_new, qt_ring, k_ring, vt_ring, k_new, vt_new, ot_scr, s_scr, m_scr,
                  xa_ring, ga_ring, gb_ring, gm_ring, o_scr, halo_scr, ext_scr, *, tiles_per_seq, seq_len):
    s = pl.program_id(0)
    tm = x_ref.shape[1]
    qb = tm // Q_BLOCK
    nq = GROUP * Q_BLOCK
    nctx = kx_ref.shape[1]

    @pl.when(s == 0)
    def _():
        for ring in (qt_ring, k_ring, vt_ring, xa_ring, ga_ring, gb_ring, gm_ring, s_scr, m_scr):
            ring[...] = jnp.zeros_like(ring)

    wq = s % 2
    rq = 1 - wq
    halo_scr[...] = xa_ring[wq, tm - POOL_HALO:tm, :]
    ck = (s + 1) % 2
    pk = s % 2
    t_a = jnp.maximum(s - 1, 0) % tiles_per_seq
    has_prev = t_a > 0
    has_next = t_a < tiles_per_seq - 1
    has_prev_early = s % tiles_per_seq > 0

    @pl.when(s == 0)
    def _():
        h_scr[...] = _norm_modulate(x0_ref[0], g_ref[...], shift0_ref[0], scale0_ref[0]).astype(BF16)

    h_new[...] = _norm_modulate(x_ref[0], g_ref[...], shift_ref[0], scale_ref[0]).astype(BF16)

    def tok(lo, hi):
        return jnp.dot(h_scr[...], wtok_ref[:, lo:hi], preferred_element_type=F32)

    def proj_k():
        kf = tok(_T_K, _T_GB)
        lane = lax.broadcasted_iota(jnp.int32, kf.shape, 1)
        partner = jnp.where((lane & (2 * ROPE_HALF - 1)) < ROPE_HALF,
                            pltpu.roll(kf, KV_WIDTH - ROPE_HALF, 1),
                            pltpu.roll(kf, ROPE_HALF, 1))
        k_new[...] = (kf * cosk_ref[...] + partner * sink_ref[...]).astype(BF16)

    def proj_v():
        vt_new[...] = _dot_t(wft_ref[ATTN_WIDTH:ATTN_WIDTH + KV_WIDTH, :], h_scr[...]).astype(BF16)

    def proj_q(r0):
        qt = _dot_t(wft_ref[r0:r0 + PROJ_CHUNK, :], h_scr[...])
        q4 = qt.reshape(PROJ_CHUNK // (2 * ROPE_HALF), 2, ROPE_HALF, tm)
        qp = jnp.concatenate([q4[:, 1:2], q4[:, 0:1]], axis=1).reshape(PROJ_CHUNK, tm)
        cosq = jnp.tile(cosq_ref[...], (PROJ_CHUNK // HEAD_DIM, 1))
        sinq = jnp.tile(sinq_ref[...], (PROJ_CHUNK // HEAD_DIM, 1))
        qt_ring[wq, r0:r0 + PROJ_CHUNK, :] = (qt * cosq + qp * sinq).astype(BF16)

    def proj_tok(ring, base, lo, act):
        v = tok(base + lo, base + lo + PROJ_CHUNK)
        ring[wq, :, lo:lo + PROJ_CHUNK] = (v if act is None else act(v)).astype(BF16)

    def tok_chunks(ring, base, width, act):
        return [functools.partial(proj_tok, ring, base, lo, act) for lo in range(0, width, PROJ_CHUNK)]

    gm_chunks = tok_chunks(gm_ring, _T_GM, 2 * D_MODEL, _sigmoid)
    chunks = gm_chunks[:PROJ_CHUNKS_BEFORE_KV] + [proj_k, proj_v] + gm_chunks[PROJ_CHUNKS_BEFORE_KV:]
    kv_done = PROJ_CHUNKS_BEFORE_KV + 2
    chunks += tok_chunks(xa_ring, _T_XA, POOL_WIDTH, None)
    xa_done = len(chunks)
    chunks += tok_chunks(gb_ring, _T_GB, ATTN_WIDTH, _silu)
    chunks += tok_chunks(ga_ring, _T_GA, POOL_WIDTH, _silu)
    chunks += [functools.partial(proj_q, r0) for r0 in range(0, ATTN_WIDTH, PROJ_CHUNK)]

    kk = lax.broadcasted_iota(jnp.int32, (Q_BLOCK, nq), 0)
    qq = lax.broadcasted_iota(jnp.int32, (Q_BLOCK, nq), 1) & (Q_BLOCK - 1)
    tri_p = kk >= qq
    tri_n = kk <= qq
    ones_rows = jnp.ones((BF16_SUBLANES, 3 * Q_BLOCK + nctx), BF16)
    kx, vx = kx_ref[0], vx_ref[0]

    def k_blk(u, early):
        if u < 0:
            return k_ring[ck if early else pk, (qb - 1) * Q_BLOCK:qb * Q_BLOCK, :]
        if early:
            return k_new[u * Q_BLOCK:(u + 1) * Q_BLOCK, :]
        if u >= qb:
            return k_new[0:Q_BLOCK, :]
        return k_ring[ck, u * Q_BLOCK:(u + 1) * Q_BLOCK, :]

    def v_blk(u, rows):
        if u < 0:
            return vt_ring[pk, rows, (qb - 1) * Q_BLOCK:qb * Q_BLOCK]
        if u >= qb:
            return vt_new[rows, 0:Q_BLOCK]
        return vt_ring[ck, rows, u * Q_BLOCK:(u + 1) * Q_BLOCK]

    def scores(u, h, slot, early=False):
        assert not early or u + 1 < qb
        q_h = jnp.concatenate(
            [qt_ring[wq if early else rq, (GROUP * h + g) * HEAD_DIM:(GROUP * h + g + 1) * HEAD_DIM,
                     u * Q_BLOCK:(u + 1) * Q_BLOCK] for g in range(GROUP)], axis=1)
        pieces = []
        if h > 0:
            pieces.append(jnp.zeros((h * HEAD_DIM, nq), BF16))
        pieces.append(q_h)
        if h < N_KV_HEADS - 1:
            pieces.append(jnp.zeros(((N_KV_HEADS - 1 - h) * HEAD_DIM, nq), BF16))
        q_pad = jnp.concatenate(pieces, axis=0)
        mask_p = tri_p & (has_prev_early if early else has_prev) if u == 0 else tri_p
        mask_n = tri_n & has_next if u == qb - 1 else tri_n
        keys = [k_blk(u - 1, early), k_blk(u, early), k_blk(u + 1, early), kxe_ref[0] if early else kx]
        sc = jnp.dot(jnp.concatenate(keys, axis=0), q_pad, preferred_element_type=F32)
        s_all = (jnp.where(mask_p, sc[0:Q_BLOCK], NEG_INF), sc[Q_BLOCK:2 * Q_BLOCK],
                 jnp.where(mask_n, sc[2 * Q_BLOCK:3 * Q_BLOCK], NEG_INF), sc[3 * Q_BLOCK:])
        m8 = functools.reduce(jnp.maximum, [jnp.max(t.reshape(-1, F32_SUBLANES, nq), axis=0) for t in s_all])
        r0 = 0
        for t in s_all:
            s_scr[slot, r0:r0 + t.shape[0], :] = t
            r0 += t.shape[0]
        m_scr[slot] = m8

    def finish(u, h, slot):
        sink = srow_ref[h]
        m = jnp.maximum(jnp.max(m_scr[slot], axis=0, keepdims=True), sink)
        p_all = jnp.exp2(s_scr[slot] - m).astype(BF16)
        rows = slice(h * HEAD_DIM, (h + 1) * HEAD_DIM)
        v_aug = jnp.concatenate([v_blk(u - 1, rows), v_blk(u, rows), v_blk(u + 1, rows), vx[rows]], axis=1)
        v_aug = jnp.concatenate([v_aug, ones_rows], axis=0)
        o_aug = jnp.dot(v_aug, p_all, preferred_element_type=F32)
        denom = o_aug[HEAD_DIM:HEAD_DIM + 1, :] + jnp.exp2(sink - m)
        o_t = o_aug[:HEAD_DIM, :] * (1.0 / denom)
        for g in range(GROUP):
            r = (GROUP * h + g) * HEAD_DIM
            ot_scr[r:r + HEAD_DIM, u * Q_BLOCK:(u + 1) * Q_BLOCK] = o_t[:, g * Q_BLOCK:(g + 1) * Q_BLOCK]

    slab = min(TAIL_SLAB, tm)
    post_scale = gate_ref[0] * gpost_ref[...]

    def fill_ext():
        ext_scr[0:POOL_HALO, :] = jnp.where(has_prev, halo_scr[...].astype(F32), 0.0)
        ext_scr[POOL_HALO:POOL_HALO + tm, :] = xa_ring[rq].astype(F32)
        ext_scr[POOL_HALO + tm:, :] = jnp.where(has_next, xa_ring[wq, 0:POOL_HALO, :].astype(F32), 0.0)

    def branch_b(r0):
        rows = slice(r0, r0 + slab)
        return jnp.dot(o_scr[rows, :] * gb_ring[rq, rows, :], wb_ref[...], preferred_element_type=F32)

    def branch_a(r0):
        pos = t_a * tm + r0 + lax.broadcasted_iota(jnp.int32, (slab, 1), 0)
        pooled = []
        for gi, w in enumerate(POOL_WINDOWS):
            cols = slice(gi * POOL_GC, (gi + 1) * POOL_GC)
            span, width = ext_scr[r0:r0 + slab + 2 * POOL_HALO, cols], 1
            while width < w // 2:
                n = span.shape[0] - width
                span = span[0:n] + span[width:width + n]
                width *= 2
            acc = span[POOL_HALO - w // 2:POOL_HALO - w // 2 + slab] + span[POOL_HALO:POOL_HALO + slab]
            cnt = (jnp.minimum(pos + w // 2, seq_len) - jnp.maximum(pos - w // 2, 0)).astype(F32)
            pooled.append((acc * (1.0 / cnt)
                           - ext_scr[POOL_HALO + r0:POOL_HALO + r0 + slab, cols]).astype(BF16))
        ys = [jnp.dot(jnp.concatenate(pooled[2 * j:2 * j + 2], axis=1), pw_ref[j],
                      preferred_element_type=F32) for j in range(len(POOL_WINDOWS) // 2)]
        y_a = jnp.concatenate(ys, axis=1) * ps_ref[...] * ga_ring[rq, r0:r0 + slab, :].astype(F32)
        return jnp.dot(y_a.astype(BF16), wa_ref[...], preferred_element_type=F32)

    def merge_out(r0, z_a, z_b):
        rows = slice(r0, r0 + slab)
        gsig = gm_ring[rq, rows, :].astype(F32)
        merged = gsig[:, :D_MODEL] * z_a + gsig[:, D_MODEL:] * z_b
        out = jnp.dot(merged.astype(BF16), wo_ref[...], preferred_element_type=F32)
        ms = jnp.mean(out * out, axis=-1, keepdims=True)
        out_ref[0, rows, :] = xres_ref[0, rows, :] + out * lax.rsqrt(ms + EPS) * post_scale

    units = [(u, h) for u in range(qb) for h in range(N_KV_HEADS)]
    nslot = s_scr.shape[0]
    ahead = nslot - 1
    spread = len(chunks) - PROJ_CHUNKS_AFTER
    issued_by = [-(-spread * (n + 1) // len(units)) for n in range(len(units))]
    assert tiles_per_seq == 1 or issued_by[(qb - 1) * N_KV_HEADS - ahead - 1] >= kv_done
    assert spread >= xa_done
    issued = 0
    for n, un in enumerate(units):
        if n + ahead < len(units):
            scores(*units[n + ahead], (n + ahead) % nslot)
        while issued < issued_by[n]:
            chunks.pop(0)()
            issued += 1
        finish(*un, n % nslot)
        if un[1] == N_KV_HEADS - 1:
            u = un[0]
            o_scr[u * Q_BLOCK:(u + 1) * Q_BLOCK, :] = ot_scr[:, u * Q_BLOCK:(u + 1) * Q_BLOCK].T.astype(BF16)

    def next_chunk():
        if chunks:
            chunks.pop(0)()

    fill_ext()
    starts = list(range(0, tm, slab))
    z_b = branch_b(starts[0])
    next_chunk()
    z_a = branch_a(starts[0])
    for i, r0 in enumerate(starts):
        next_chunk()
        if i + 1 < len(starts):
            z_b_next = branch_b(starts[i + 1])
            next_chunk()
            z_a_next = branch_a(starts[i + 1])
            next_chunk()
        merge_out(r0, z_a, z_b)
        if i + 1 < len(starts):
            z_a, z_b = z_a_next, z_b_next
    for chunk in chunks:
        chunk()
    h_scr[...] = h_new[...]
    for n, un in enumerate(units[:ahead]):
        scores(*un, n % nslot, early=True)
    k_ring[pk] = k_new[...]
    vt_ring[pk] = vt_new[...]


def _layer(x, shift, scale, gate, g, wtok, wft, cosk, sink, cosq, sinq, kx, vxt, sink_rows,
           pw_pairs, pool_scale, wa, wb, wo, gpost, tm):
    B, L, _ = x.shape
    C = kx.shape[1]
    tps = L // tm
    n_tiles = B * tps

    def p_tile(s):
        tile = jnp.minimum(s, n_tiles - 1)
        return tile // tps, tile % tps

    def a_tile(s):
        tile = jnp.maximum(s - 1, 0)
        return tile // tps, tile % tps

    tok_p = lambda s: (*p_tile(s), 0)
    tok_a = lambda s: (*a_tile(s), 0)
    vec_p = lambda s: (p_tile(s)[0], 0, 0)
    vec_a = lambda s: (a_tile(s)[0], 0, 0)
    tok_n = lambda s: (*p_tile(s + 1), 0)
    vec_n = lambda s: (p_tile(s + 1)[0], 0, 0)
    return pl.pallas_call(
        functools.partial(_layer_kernel, tiles_per_seq=tps, seq_len=L),
        grid=(n_tiles + 1,),
        in_specs=[pl.BlockSpec((1, tm, D_MODEL), tok_n),
                  pl.BlockSpec((1, 1, D_MODEL), vec_n),
                  pl.BlockSpec((1, 1, D_MODEL), vec_n),
                  _const_spec((1, tm, D_MODEL)),
                  _const_spec((1, 1, D_MODEL)),
                  _const_spec((1, 1, D_MODEL)),
                  _const_spec((1, D_MODEL)),
                  _const_spec((D_MODEL, _T_END)),
                  _const_spec((ATTN_WIDTH + KV_WIDTH, D_MODEL)),
                  pl.BlockSpec((tm, KV_WIDTH), lambda s: (p_tile(s)[1], 0)),
                  pl.BlockSpec((tm, KV_WIDTH), lambda s: (p_tile(s)[1], 0)),
                  pl.BlockSpec((HEAD_DIM, tm), lambda s: (0, p_tile(s)[1])),
                  pl.BlockSpec((HEAD_DIM, tm), lambda s: (0, p_tile(s)[1])),
                  pl.BlockSpec((1, C, KV_WIDTH), vec_a),
                  pl.BlockSpec((1, KV_WIDTH, C), vec_a),
                  pl.BlockSpec((1, C, KV_WIDTH), vec_p),
                  _const_spec((N_KV_HEADS, 1, GROUP * Q_BLOCK)),
                  pl.BlockSpec((1, tm, D_MODEL), tok_a),
                  pl.BlockSpec((1, 1, D_MODEL), vec_a),
                  _const_spec((len(POOL_WINDOWS) // 2, 2 * POOL_GC, 2 * POOL_GC)),
                  _const_spec((1, POOL_WIDTH)),
                  _const_spec((POOL_WIDTH, D_MODEL)),
                  _const_spec((ATTN_WIDTH, D_MODEL)),
                  _const_spec((D_MODEL, D_MODEL)),
                  _const_spec((1, D_MODEL))],
        out_specs=pl.BlockSpec((1, tm, D_MODEL), tok_a),
        out_shape=jax.ShapeDtypeStruct((B, L, D_MODEL), F32),
        scratch_shapes=[pltpu.VMEM((tm, D_MODEL), BF16),
                        pltpu.VMEM((tm, D_MODEL), BF16),
                        pltpu.VMEM((2, ATTN_WIDTH, tm), BF16),
                        pltpu.VMEM((2, tm, KV_WIDTH), BF16),
                        pltpu.VMEM((2, KV_WIDTH, tm), BF16),
                        pltpu.VMEM((tm, KV_WIDTH), BF16),
                        pltpu.VMEM((KV_WIDTH, tm), BF16),
                        pltpu.VMEM((ATTN_WIDTH, tm), F32),
                        pltpu.VMEM((ATTN_SCORE_SLOTS, 3 * Q_BLOCK + C, GROUP * Q_BLOCK), F32),
                        pltpu.VMEM((ATTN_SCORE_SLOTS, F32_SUBLANES, GROUP * Q_BLOCK), F32),
                        pltpu.VMEM((2, tm, POOL_WIDTH), BF16),
                        pltpu.VMEM((2, tm, POOL_WIDTH), BF16),
                        pltpu.VMEM((2, tm, ATTN_WIDTH), BF16),
                        pltpu.VMEM((2, tm, 2 * D_MODEL), BF16),
                        pltpu.VMEM((tm, ATTN_WIDTH), BF16),
                        pltpu.VMEM((POOL_HALO, POOL_WIDTH), BF16),
                        pltpu.VMEM((tm + 2 * POOL_HALO, POOL_WIDTH), F32)],
        compiler_params=pltpu.CompilerParams(dimension_semantics=("arbitrary",),
                                             vmem_limit_bytes=LAYER_VMEM_LIMIT_BYTES),
        name="layer",
    )(x, shift, scale, x, shift, scale, g, wtok, wft, cosk, sink, cosq, sinq, kx, vxt, kx, sink_rows,
      x, gate, pw_pairs, pool_scale, wa, wb, wo, gpost)


def _rope_tables(L):
    rows = L // GRID_W
    row = jnp.repeat(jnp.arange(rows), GRID_W).astype(F32)
    col = jnp.tile(jnp.arange(GRID_W), rows).astype(F32)
    freqs = ROPE_BASE ** (-jnp.arange(ROPE_HALF, dtype=F32) / ROPE_HALF)
    ang_r = row[:, None] * freqs[None, :]
    ang_c = col[:, None] * freqs[None, :]
    cos = jnp.concatenate([jnp.cos(ang_r), jnp.cos(ang_r), jnp.cos(ang_c), jnp.cos(ang_c)], axis=1)
    sin = jnp.concatenate([-jnp.sin(ang_r), jnp.sin(ang_r), -jnp.sin(ang_c), jnp.sin(ang_c)], axis=1)
    return cos, sin


def kernel(x, c, ctx, c_ctx, w_mod, b_mod, norm_pre_g, norm_post_g, w_in, pool_w, pool_scale, sink,
           w_branch_a, w_branch_b, w_out):
    B, L, D = x.shape
    depth = w_in.shape[0]
    assert D == D_MODEL and L % Q_BLOCK == 0 and L % GRID_W == 0 and w_in.shape[2] == IN_WIDTH
    assert depth == 1, "context-stream update between layers is not implemented"
    tm = min(PROJ_ROWS, L)

    cos, sin = _rope_tables(L)
    cosk = jnp.tile(cos, (1, N_KV_HEADS))
    sink_k = jnp.tile(sin, (1, N_KV_HEADS))
    qscale = HEAD_DIM ** -0.5 * LOG2_E
    cosq = (cos * qscale).T
    sinq = (sin * qscale).T

    for i in range(depth):
        pad = (-(B + 1)) % 8
        c_all = jnp.concatenate([c, c_ctx[None, :], jnp.zeros((pad, D), F32)], axis=0)
        mod = _adaln(c_all, w_mod[i], b_mod[i][None, :])
        shift, scale, gate = (mod[:B, j * D:(j + 1) * D][:, None, :] for j in range(3))
        c_shift, c_scale = (mod[B:B + 1, j * D:(j + 1) * D] for j in range(2))

        wtok, wft = _weight_slabs(w_in[i])
        g_pre = norm_pre_g[i][None, :]

        kx, vxt = _ctx_kv(ctx, g_pre, c_shift, c_scale, wtok[:, _T_K:_T_GB], wft[ATTN_WIDTH:])
        sink_rows = jnp.repeat(sink[i].astype(F32) * LOG2_E, Q_BLOCK).reshape(N_KV_HEADS, 1, GROUP * Q_BLOCK)
        pw = pool_w[i].astype(BF16)
        zero = jnp.zeros((POOL_GC, POOL_GC), BF16)
        pw_pairs = jnp.stack([jnp.block([[pw[2 * j], zero], [zero, pw[2 * j + 1]]])
                              for j in range(len(POOL_WINDOWS) // 2)])
        x = _layer(x, shift, scale, gate, g_pre, wtok, wft, cosk, sink_k, cosq, sinq, kx, vxt, sink_rows,
                   pw_pairs, pool_scale[i][None, :],
                   w_branch_a[i].astype(BF16), w_branch_b[i].astype(BF16), w_out[i].astype(BF16),
                   norm_post_g[i][None, :], tm)
    return x
```
